```python
import jax, jax.numpy as jnp
from jax import lax
import numpy as np

D_MODEL = 2048
BATCH = 2
SEQ = 8192
DEPTH = 4

PLE_DIM = 256
D_FF = 5504
EPS = 1e-6
NEG_INF = -1e30
N_BRANCH = 4
BRANCH_WIDTH = 1024

RNN_WIDTH = 1024
RNN_HEADS = 16
RNN_HEAD_DIM = RNN_WIDTH // RNN_HEADS
CONV_WIDTH = 4
RG_C = 8.0

POOL_WIDTH = 1024
POOL_WINDOWS = (2, 4, 8, 16)
POOL_GROUP = POOL_WIDTH // len(POOL_WINDOWS)

HEAD_DIM = 128
KV_HEADS = 8
ATTN_PATTERNS = ((128, 1), (512, 4), (2048, 16))
N_GROUPS_C = len(ATTN_PATTERNS)
Q_HEADS = KV_HEADS * N_GROUPS_C
ATTN_BLOCK = 128
ATTN_WIDTH = KV_HEADS * HEAD_DIM

SG_WIDTH = 1024
SG_CHUNK = 128
SG_GROUPS = 8
SG_GROUP_DIM = SG_WIDTH // SG_GROUPS

IN_WIDTHS = (RNN_WIDTH, RNN_WIDTH, POOL_WIDTH, Q_HEADS * HEAD_DIM, ATTN_WIDTH, ATTN_WIDTH,
             2 * SG_WIDTH, N_BRANCH * D_MODEL)
IN_COLS = sum(IN_WIDTHS)

kernel_name = "hybrid_parallel_gated_trunk"


def rms_norm(x, g):
    xf = x.astype(jnp.float32)
    y = xf * lax.rsqrt(jnp.mean(xf * xf, axis=-1, keepdims=True) + EPS)
    return (y * g.astype(jnp.float32)).astype(x.dtype)


def swiglu(h, w1, w3, w2):
    return (jax.nn.silu(h @ w1) * (h @ w3)) @ w2


def causal_depthwise_conv(x, w, b):
    S = x.shape[1]
    xp = jnp.pad(x, ((0, 0), (CONV_WIDTH - 1, 0), (0, 0)))
    y = b
    for j in range(CONV_WIDTH):
        y = y + w[j] * xp[:, CONV_WIDTH - 1 - j: CONV_WIDTH - 1 - j + S]
    return y


def rg_lru(x, wa, ba, wx, bx, lam):
    B, S, _ = x.shape
    xh = x.reshape(B, S, RNN_HEADS, RNN_HEAD_DIM)
    r = jax.nn.sigmoid(jnp.einsum('bshi,hij->bshj', xh, wa).reshape(B, S, RNN_WIDTH) + ba)
    i_g = jax.nn.sigmoid(jnp.einsum('bshi,hij->bshj', xh, wx).reshape(B, S, RNN_WIDTH) + bx)
    log_a = (-RG_C * r.astype(jnp.float32)) * jax.nn.softplus(-lam.astype(jnp.float32))
    a = jnp.exp(log_a)
    mult = jnp.sqrt(-jnp.expm1(2.0 * log_a))
    u = mult * (i_g * x).astype(jnp.float32)

    def combine(c1, c2):
        a1, b1 = c1
        a2, b2 = c2
        return a1 * a2, a2 * b1 + b2

    _, h = lax.associative_scan(combine, (a, u), axis=1)
    return h.astype(x.dtype)


def multi_scale_pool(x, w_pool, scale):
    B, S, _ = x.shape
    xg = x.reshape(B, S, len(POOL_WINDOWS), POOL_GROUP).astype(jnp.float32)
    csum = jnp.cumsum(xg, axis=1)
    pos = jnp.arange(S)
    outs = []
    for g, win in enumerate(POOL_WINDOWS):
        c = csum[:, :, g]
        shifted = jnp.pad(c, ((0, 0), (win, 0), (0, 0)))[:, :S]
        cnt = jnp.minimum(pos + 1, win).astype(jnp.float32)[None, :, None]
        outs.append((c - shifted) / cnt - xg[:, :, g])
    pooled = jnp.stack(outs, axis=2)
    y = jnp.einsum('bsgc,gcd->bsgd', pooled, w_pool.astype(jnp.float32)).reshape(B, S, POOL_WIDTH)
    return (y * scale.astype(jnp.float32)).astype(x.dtype)


def dilated_window_attention(q, k, v, window, dilation):
    B, S, H, D = q.shape
    n_back = window // dilation
    assert n_back <= ATTN_BLOCK
    span = dilation * ATTN_BLOCK
    S_pad = -(-S // span) * span
    L = S_pad // dilation
    nb = L // ATTN_BLOCK
    pad = ((0, 0), (0, S_pad - S), (0, 0), (0, 0))

    def to_blocks(t):
        t = jnp.pad(t, pad).reshape(B, L, dilation, H, D).transpose(0, 2, 1, 3, 4)
        return t.reshape(B, dilation, nb, ATTN_BLOCK, H, D)

    def band(t):
        prev = jnp.pad(t, ((0, 0), (0, 0), (1, 0), (0, 0), (0, 0), (0, 0)))[:, :, :-1]
        return jnp.concatenate([prev, t], axis=3)

    qb = to_blocks(q)
    kband = band(to_blocks(k))
    vband = band(to_blocks(v))
    s = jnp.einsum('brnqhd,brnkhd->brnhqk', qb, kband) * (D ** -0.5)
    qi = jnp.arange(ATTN_BLOCK)[:, None] + ATTN_BLOCK
    ki = jnp.arange(2 * ATTN_BLOCK)[None, :]
    dist = qi - ki
    key_pos = jnp.arange(nb)[:, None] * ATTN_BLOCK + ki - ATTN_BLOCK
    mask = ((dist >= 0) & (dist <= n_back))[None] & (key_pos >= 0)[:, None, :]
    s = jnp.where(mask[None, None, :, None], s, NEG_INF)
    m = jnp.max(s, axis=-1, keepdims=True)
    pexp = jnp.exp(s - m)
    denom = jnp.sum(pexp, axis=-1, keepdims=True)
    o = jnp.einsum('brnhqk,brnkhd->brnqhd', pexp, vband) / denom.transpose(0, 1, 2, 4, 3, 5)
    lse = (m + jnp.log(denom))[..., 0].transpose(0, 1, 2, 4, 3)
    o = o.reshape(B, dilation, L, H, D).transpose(0, 2, 1, 3, 4).reshape(B, S_pad, H, D)[:, :S]
    lse = lse.reshape(B, dilation, L, H).transpose(0, 2, 1, 3).reshape(B, S_pad, H)[:, :S]
    return o, lse


def spatial_gating(z, norm_g, w_s, b_s):
    u, vv = jnp.split(z, 2, axis=-1)
    vv = rms_norm(vv, norm_g)
    B, S, _ = vv.shape
    nc = S // SG_CHUNK
    vc = vv.reshape(B, nc, SG_CHUNK, SG_GROUPS, SG_GROUP_DIM)
    tri = jnp.tril(jnp.ones((SG_CHUNK, SG_CHUNK), dtype=bool))
    ws = jnp.where(tri[None], w_s, 0.0)
    mixed = jnp.einsum('gts,bnsgc->bntgc', ws, vc) + b_s.T[None, None, :, :, None]
    return u * mixed.reshape(B, S, SG_WIDTH)


def setup_inputs(seed: int = 0) -> dict:
    key = jax.random.key(seed)
    ks = iter(jax.random.split(key, 40))

    def nrm(shape, scale):
        return jax.random.normal(next(ks), shape, jnp.float32) * scale

    def gain(shape, s=0.02):
        return 1.0 + nrm(shape, s)

    L, D, F = DEPTH, D_MODEL, D_FF
    x = nrm((BATCH, SEQ, D), 1.0)
    p = nrm((L, BATCH, SEQ, PLE_DIM), 1.0)
    ffn1_norm = gain((L, D))
    ffn1_w1 = nrm((L, D, F), D ** -0.5)
    ffn1_w3 = nrm((L, D, F), D ** -0.5)
    ffn1_w2 = nrm((L, F, D), F ** -0.5)
    mix_norm = gain((L, D))
    w_in = nrm((L, D, IN_COLS), D ** -0.5)
    b_gate = nrm((L, N_BRANCH, D), 0.02)
    conv_w = nrm((L, CONV_WIDTH, RNN_WIDTH), CONV_WIDTH ** -0.5)
    conv_b = nrm((L, RNN_WIDTH), 0.02)
    rg_wa = nrm((L, RNN_HEADS, RNN_HEAD_DIM, RNN_HEAD_DIM), RNN_HEAD_DIM ** -0.5)
    rg_ba = nrm((L, RNN_WIDTH), 0.02)
    rg_wx = nrm((L, RNN_HEADS, RNN_HEAD_DIM, RNN_HEAD_DIM), RNN_HEAD_DIM ** -0.5)
    rg_bx = nrm((L, RNN_WIDTH), 0.02)
    a0 = jax.random.uniform(next(ks), (L, RNN_WIDTH), jnp.float32, 0.9, 0.999)
    rg_lambda = jnp.log(a0) - jnp.log1p(-a0)
    pool_w = nrm((L, len(POOL_WINDOWS), POOL_GROUP, POOL_GROUP), POOL_GROUP ** -0.5)
    pool_scale = gain((L, POOL_WIDTH), 0.1)
    q_gain = gain((L, HEAD_DIM))
    k_gain = gain((L, HEAD_DIM))
    sg_norm = gain((L, SG_WIDTH))
    sg_w = nrm((L, SG_GROUPS, SG_CHUNK, SG_CHUNK), SG_CHUNK ** -0.5)
    sg_b = gain((L, SG_GROUPS, SG_CHUNK))
    w_branch = nrm((L, N_BRANCH, BRANCH_WIDTH, D), BRANCH_WIDTH ** -0.5)
    w_out = nrm((L, D, D), D ** -0.5)
    ffn2_norm = gain((L, D))
    ffn2_w1 = nrm((L, D, F), D ** -0.5)
    ffn2_w3 = nrm((L, D, F), D ** -0.5)
    ffn2_w2 = nrm((L, F, D), F ** -0.5)
    ple_norm = gain((L, D))
    ple_gate_w = nrm((L, D, D), D ** -0.5)
    ple_proj = nrm((L, PLE_DIM, D), PLE_DIM ** -0.5)
    return {"x": x, "p": p, "ffn1_norm": ffn1_norm, "ffn1_w1": ffn1_w1, "ffn1_w3": ffn1_w3,
            "ffn1_w2": ffn1_w2, "mix_norm": mix_norm, "w_in": w_in, "b_gate": b_gate,
            "conv_w": conv_w, "conv_b": conv_b, "rg_wa": rg_wa, "rg_ba": rg_ba, "rg_wx": rg_wx,
            "rg_bx": rg_bx, "rg_lambda": rg_lambda, "pool_w": pool_w, "pool_scale": pool_scale,
            "q_gain": q_gain, "k_gain": k_gain, "sg_norm": sg_norm, "sg_w": sg_w, "sg_b": sg_b,
            "w_branch": w_branch, "w_out": w_out, "ffn2_norm": ffn2_norm, "ffn2_w1": ffn2_w1,
            "ffn2_w3": ffn2_w3, "ffn2_w2": ffn2_w2, "ple_norm": ple_norm, "ple_gate_w": ple_gate_w,
            "ple_proj": ple_proj}


def reference(x, p, ffn1_norm, ffn1_w1, ffn1_w3, ffn1_w2, mix_norm, w_in, b_gate, conv_w, conv_b,
              rg_wa, rg_ba, rg_wx, rg_bx, rg_lambda, pool_w, pool_scale, q_gain, k_gain, sg_norm,
              sg_w, sg_b, w_branch, w_out, ffn2_norm, ffn2_w1, ffn2_w3, ffn2_w2, ple_norm,
              ple_gate_w, ple_proj):
    B, S, _ = x.shape
    split_idx = np.cumsum(IN_WIDTHS)[:-1].tolist()
    for i in range(DEPTH):
        x = x + 0.5 * swiglu(rms_norm(x, ffn1_norm[i]), ffn1_w1[i], ffn1_w3[i], ffn1_w2[i])

        h = rms_norm(x, mix_norm[i])
        proj = h @ w_in[i]
        xa, ga, xb, qf, kf, vf, zd, gates = jnp.split(proj, split_idx, axis=-1)

        ya = rg_lru(causal_depthwise_conv(xa, conv_w[i], conv_b[i]),
                    rg_wa[i], rg_ba[i], rg_wx[i], rg_bx[i], rg_lambda[i]) * jax.nn.gelu(ga)

        yb = multi_scale_pool(xb, pool_w[i], pool_scale[i])

        q = rms_norm(qf.reshape(B, S, Q_HEADS, HEAD_DIM), q_gain[i]).astype(jnp.float32)
        q = q.reshape(B, S, N_GROUPS_C, KV_HEADS, HEAD_DIM)
        k = rms_norm(kf.reshape(B, S, KV_HEADS, HEAD_DIM), k_gain[i]).astype(jnp.float32)
        v = vf.reshape(B, S, KV_HEADS, HEAD_DIM).astype(jnp.float32)
        outs, lses = [], []
        for g, (win, dil) in enumerate(ATTN_PATTERNS):
            o, l = dilated_window_attention(q[:, :, g], k, v, win, dil)
            outs.append(o)
            lses.append(l)
        wg = jax.nn.softmax(jnp.stack(lses, axis=0), axis=0)[..., None]
        yc = jnp.sum(wg * jnp.stack(outs, axis=0), axis=0).reshape(B, S, ATTN_WIDTH).astype(x.dtype)

        yd = spatial_gating(jax.nn.gelu(zd), sg_norm[i], sg_w[i], sg_b[i])

        gate = jax.nn.sigmoid(gates.reshape(B, S, N_BRANCH, D_MODEL) + b_gate[i])
        merged = gate[:, :, 0] * (ya @ w_branch[i, 0])
        merged = merged + gate[:, :, 1] * (yb @ w_branch[i, 1])
        merged = merged + gate[:, :, 2] * (yc @ w_branch[i, 2])
        merged = merged + gate[:, :, 3] * (yd @ w_branch[i, 3])
        x = x + merged @ w_out[i]

        x = x + 0.5 * swiglu(rms_norm(x, ffn2_norm[i]), ffn2_w1[i], ffn2_w3[i], ffn2_w2[i])

        pg = jax.nn.sigmoid(rms_norm(x, ple_norm[i]) @ ple_gate_w[i])
        x = x + pg * (p[i] @ ple_proj[i])
    return x
```

```python
import functools

import jax
import jax.numpy as jnp
from jax import lax
from jax.experimental import pallas as pl
from jax.experimental.pallas import tpu as pltpu

EPS = 1e-6
NEG_INF = -1e30
D_MODEL = 2048
D_FF = 5504
LANES = 128
FF_TILE = 512
D_FF_PAD = -(-D_FF // FF_TILE) * FF_TILE
PLE_DIM = 256
WIDTH = 1024
RNN_HEADS = 16
RNN_HEAD_DIM = 64
RNN_BLOCK = 256
RG_C = 8.0
CONV_WIDTH = 4
POOL_WINDOWS = (2, 4, 8, 16)
POOL_GROUP = 256
POOL_HALO = 16
CONV_HALO = 8
HEAD_DIM = 128
KV_HEADS = 8
ATTN_DILATIONS = (1, 4, 16)
ATTN_BLOCK = 128
ATTN_TILE = ATTN_BLOCK * max(ATTN_DILATIONS)
SG_CHUNK = 128
SG_GROUPS = 8
N_BRANCH = 4
COL_XA, COL_GA, COL_XB, COL_Q, COL_K, COL_V, COL_ZD, COL_GATES = (
    0, 1024, 2048, 3072, 6144, 7168, 8192, 10240)
MIX_COLS = COL_GATES
VMEM_LIMIT = 56 * 1024 * 1024

_BF = jnp.bfloat16
_F32 = jnp.float32


def _params(*sem):
    return pltpu.CompilerParams(dimension_semantics=sem, vmem_limit_bytes=VMEM_LIMIT)


def _rms(x, g):
    return x * lax.rsqrt(jnp.mean(x * x, axis=-1, keepdims=True) + EPS) * g


def _dot(a, b):
    return jnp.dot(a, b, preferred_element_type=_F32)


def _ffn_kernel(x_ref, g_ref, w1_ref, w3_ref, w2_ref, o_ref, h_ref):
    @pl.when(pl.program_id(1) == 0)
    def _():
        x = x_ref[...]
        h_ref[...] = _rms(x, g_ref[...]).astype(_BF)
        o_ref[...] = x

    h = h_ref[...]
    a = _dot(h, w1_ref[...])
    b = _dot(h, w3_ref[...])
    act = (0.5 * (a * jax.nn.sigmoid(a)) * b).astype(_BF)
    o_ref[...] += _dot(act, w2_ref[...])


def _ffn(x2, norm, w1, w3, w2, layer, tm):
    T, D = x2.shape
    tf = FF_TILE
    return pl.pallas_call(
        _ffn_kernel,
        grid=(T // tm, w1.shape[2] // tf),
        in_specs=[
            pl.BlockSpec((tm, D), lambda i, j: (i, 0)),
            pl.BlockSpec((None, 1, D), lambda i, j: (layer, 0, 0)),
            pl.BlockSpec((None, D, tf), lambda i, j: (layer, 0, j)),
            pl.BlockSpec((None, D, tf), lambda i, j: (layer, 0, j)),
            pl.BlockSpec((None, tf, D), lambda i, j: (layer, j, 0)),
        ],
        out_specs=pl.BlockSpec((tm, D), lambda i, j: (i, 0)),
        out_shape=jax.ShapeDtypeStruct((T, D), _F32),
        scratch_shapes=[pltpu.VMEM((tm, D), _BF)],
        compiler_params=_params("parallel", "arbitrary"),
        name="ffn",
    )(x2, norm, w1, w3, w2)


def _proj_kernel(x_ref, g_ref, w_ref, gain_ref, o_ref, h_ref, *, norm_lo, norm_hi):
    j = pl.program_id(1)

    @pl.when(j == 0)
    def _():
        h_ref[...] = _rms(x_ref[...], g_ref[...]).astype(_BF)

    acc = _dot(h_ref[...], w_ref[...])
    is_qk = jnp.logical_and(j >= norm_lo, j < norm_hi)

    @pl.when(is_qk)
    def _():
        for c in range(acc.shape[1] // HEAD_DIM):
            sl = slice(c * HEAD_DIM, (c + 1) * HEAD_DIM)
            o_ref[:, sl] = _rms(acc[:, sl], gain_ref[:, sl])

    @pl.when(jnp.logical_not(is_qk))
    def _():
        o_ref[...] = acc


def _proj(x2, norm, w_in, gain, layer, tm, tn):
    T, D = x2.shape
    return pl.pallas_call(
        functools.partial(_proj_kernel, norm_lo=COL_Q // tn, norm_hi=COL_V // tn),
        grid=(T // tm, MIX_COLS // tn),
        in_specs=[
            pl.BlockSpec((tm, D), lambda i, j: (i, 0)),
            pl.BlockSpec((None, 1, D), lambda i, j: (layer, 0, 0)),
            pl.BlockSpec((None, D, tn), lambda i, j: (layer, 0, j)),
            pl.BlockSpec((None, 1, tn), lambda i, j: (layer, 0, j)),
        ],
        out_specs=pl.BlockSpec((tm, tn), lambda i, j: (i, j)),
        out_shape=jax.ShapeDtypeStruct((T, MIX_COLS), _F32),
        scratch_shapes=[pltpu.VMEM((tm, D), _BF)],
        compiler_params=_params("parallel", "arbitrary"),
        name="proj",
    )(x2, norm, w_in, gain)


def _rglru_kernel(xa_ref, halo_ref, ga_ref, cw_ref, cb_ref, wa_ref, ba_ref, wx_ref, bx_ref,
                  lam_ref, o_ref, carry_ref, *, tb):
    i = pl.program_id(1)

    @pl.when(i == 0)
    def _():
        carry_ref[...] = jnp.zeros_like(carry_ref)

    x = xa_ref[...]
    halo = jnp.where(i == 0, 0.0, halo_ref[...])
    xe = jnp.concatenate([halo, x], axis=0)
    cw = cw_ref[...]
    y = cb_ref[...] + cw[0:1] * xe
    for j in range(1, CONV_WIDTH):
        y = y + cw[j:j + 1] * pltpu.roll(xe, j, axis=0)
    xc = y[CONV_HALO:]
    xcb = xc.astype(_BF)
    r_lin, i_lin = [], []
    for p in range(WIDTH // RNN_BLOCK):
        sl = slice(p * RNN_BLOCK, (p + 1) * RNN_BLOCK)
        r_lin.append(_dot(xcb[:, sl], wa_ref[p]))
        i_lin.append(_dot(xcb[:, sl], wx_ref[p]))
    r = jax.nn.sigmoid(jnp.concatenate(r_lin, axis=1) + ba_ref[...])
    ig = jax.nn.sigmoid(jnp.concatenate(i_lin, axis=1) + bx_ref[...])
    z = -lam_ref[...]
    softplus = jnp.maximum(z, 0.0) + jnp.log1p(jnp.exp(-jnp.abs(z)))
    log_a = (-RG_C * r) * softplus
    a = jnp.exp(log_a)
    b = jnp.sqrt(1.0 - a * a) * (ig * xc)
    row = lax.broadcasted_iota(jnp.int32, (tb, 1), 0)
    s = 1
    while s < tb:
        keep = row >= s
        a_prev = jnp.where(keep, pltpu.roll(a, s, axis=0), 1.0)
        b_prev = jnp.where(keep, pltpu.roll(b, s, axis=0), 0.0)
        b = a * b_prev + b
        a = a * a_prev
        s *= 2
    h = a * carry_ref[0:1, :] + b
    carry_ref[0:1, :] = h[tb - 1:tb, :]
    o_ref[...] = (h * jax.nn.gelu(ga_ref[...])).astype(_BF)


def _rglru(proj, cw, cb, wa, ba, wx, bx, lam, layer, B, S, tb):
    nb = S // tb
    hb = tb // CONV_HALO
    vec = pl.BlockSpec((None, 1, WIDTH), lambda b, i: (layer, 0, 0))
    mat = pl.BlockSpec((None, WIDTH // RNN_BLOCK, RNN_BLOCK, RNN_BLOCK), lambda b, i: (layer, 0, 0, 0))
    return pl.pallas_call(
        functools.partial(_rglru_kernel, tb=tb),
        grid=(B, nb),
        in_specs=[
            pl.BlockSpec((tb, WIDTH), lambda b, i: (b * nb + i, COL_XA // WIDTH)),
            pl.BlockSpec((CONV_HALO, WIDTH),
                         lambda b, i: (jnp.maximum((b * nb + i) * hb - 1, 0), COL_XA // WIDTH)),
            pl.BlockSpec((tb, WIDTH), lambda b, i: (b * nb + i, COL_GA // WIDTH)),
            pl.BlockSpec((None, CONV_WIDTH, WIDTH), lambda b, i: (layer, 0, 0)),
            vec, mat, vec, mat, vec, vec,
        ],
        out_specs=pl.BlockSpec((tb, WIDTH), lambda b, i: (b * nb + i, 0)),
        out_shape=jax.ShapeDtypeStruct((B * S, WIDTH), _BF),
        scratch_shapes=[pltpu.VMEM((8, WIDTH), _F32)],
        compiler_params=_params("parallel", "arbitrary"),
        name="rglru",
    )(proj, proj, proj, cw, cb, wa, ba, wx, bx, lam)


def _pool_kernel(xb_ref, halo_ref, pw_ref, sc_ref, o_ref, *, tb):
    i = pl.program_id(1)
    x = xb_ref[...]
    halo = jnp.where(i == 0, 0.0, halo_ref[...])
    xe = jnp.concatenate([halo, x], axis=0)
    pos = i * tb + lax.broadcasted_iota(jnp.int32, (tb, 1), 0)
    for g, win in enumerate(POOL_WINDOWS):
        sl = slice(g * POOL_GROUP, (g + 1) * POOL_GROUP)
        s = xe[:, sl]
        sh = 1
        while sh < win:
            s = s + pltpu.roll(s, sh, axis=0)
            sh *= 2
        cnt = jnp.minimum(pos + 1, win).astype(_F32)
        pooled = s[POOL_HALO:] / cnt - x[:, sl]
        y = _dot(pooled.astype(_BF), pw_ref[g])
        o_ref[:, sl] = (y * sc_ref[:, sl]).astype(_BF)


def _pool(proj, pw, sc, layer, B, S, tb):
    nb = S // tb
    hb = tb // POOL_HALO
    return pl.pallas_call(
        functools.partial(_pool_kernel, tb=tb),
        grid=(B, nb),
        in_specs=[
            pl.BlockSpec((tb, WIDTH), lambda b, i: (b * nb + i, COL_XB // WIDTH)),
            pl.BlockSpec((POOL_HALO, WIDTH),
                         lambda b, i: (jnp.maximum((b * nb + i) * hb - 1, 0), COL_XB // WIDTH)),
            pl.BlockSpec((None, len(POOL_WINDOWS), POOL_GROUP, POOL_GROUP), lambda b, i: (layer, 0, 0, 0)),
            pl.BlockSpec((None, 1, WIDTH), lambda b, i: (layer, 0, 0)),
        ],
        out_specs=pl.BlockSpec((tb, WIDTH), lambda b, i: (b * nb + i, 0)),
        out_shape=jax.ShapeDtypeStruct((B * S, WIDTH), _BF),
        compiler_params=_params("parallel", "parallel"),
        name="pool",
    )(proj, proj, pw, sc)


def _attn_kernel(q0_ref, q1_ref, q2_ref, kc_ref, kp_ref, vc_ref, vp_ref, o_ref,
                 ks_ref, vs_ref, o0_ref, o1_ref, o2_ref, l0_ref, l1_ref, l2_ref):
    n = pl.program_id(2)
    TQ = ATTN_TILE
    ks_ref[TQ:, :] = kc_ref[...]
    vs_ref[TQ:, :] = vc_ref[...]

    @pl.when(n == 0)
    def _():
        ks_ref[:TQ, :] = jnp.zeros((TQ, HEAD_DIM), _F32)
        vs_ref[:TQ, :] = jnp.zeros((TQ, HEAD_DIM), _F32)

    @pl.when(n > 0)
    def _():
        ks_ref[:TQ, :] = kp_ref[...]
        vs_ref[:TQ, :] = vp_ref[...]

    row = lax.broadcasted_iota(jnp.int32, (ATTN_BLOCK, 2 * ATTN_BLOCK), 0)
    col = lax.broadcasted_iota(jnp.int32, (ATTN_BLOCK, 2 * ATTN_BLOCK), 1)
    band = jnp.logical_and(col >= row, col <= row + ATTN_BLOCK)
    in_seq = col >= ATTN_BLOCK
    scale = HEAD_DIM ** -0.5

    for d, q_ref, og_ref, lg_ref in ((ATTN_DILATIONS[0], q0_ref, o0_ref, l0_ref),
                                     (ATTN_DILATIONS[1], q1_ref, o1_ref, l1_ref),
                                     (ATTN_DILATIONS[2], q2_ref, o2_ref, l2_ref)):
        stride = None if d == 1 else d

        def body(idx, carry, d=d, q_ref=q_ref, og_ref=og_ref, lg_ref=lg_ref, stride=stride):
            r = idx % d
            m = idx // d
            qs = m * (ATTN_BLOCK * d) + r
            kst = TQ + qs - ATTN_BLOCK * d
            q = q_ref[pl.ds(qs, ATTN_BLOCK, stride=stride), :].astype(_BF)
            k = ks_ref[pl.ds(kst, 2 * ATTN_BLOCK, stride=stride), :].astype(_BF)
            v = vs_ref[pl.ds(kst, 2 * ATTN_BLOCK, stride=stride), :].astype(_BF)
            s = lax.dot_general(q, k, (((1,), (1,)), ((), ())), preferred_element_type=_F32) * scale
            first = jnp.logical_and(n == 0, m == 0)
            valid = jnp.logical_and(band, jnp.logical_or(in_seq, jnp.logical_not(first)))
            s = jnp.where(valid, s, NEG_INF)
            mx = jnp.max(s, axis=-1, keepdims=True)
            p = jnp.exp(s - mx)
            l = jnp.sum(p, axis=-1, keepdims=True)
            o = _dot(p.astype(_BF), v) / l
            lse = mx + jnp.log(l)
            og_ref[pl.ds(qs, ATTN_BLOCK, stride=stride), :] = o
            lg_ref[pl.ds(qs, ATTN_BLOCK, stride=stride), :] = jnp.broadcast_to(lse, (ATTN_BLOCK, HEAD_DIM))
            return carry

        lax.fori_loop(0, TQ // ATTN_BLOCK, body, 0)

    l0, l1, l2 = l0_ref[...], l1_ref[...], l2_ref[...]
    mx = jnp.maximum(jnp.maximum(l0, l1), l2)
    w0, w1, w2 = jnp.exp(l0 - mx), jnp.exp(l1 - mx), jnp.exp(l2 - mx)
    out = (w0 * o0_ref[...] + w1 * o1_ref[...] + w2 * o2_ref[...]) / (w0 + w1 + w2)
    o_ref[...] = out.astype(_BF)


def _attn(proj, B, S):
    TQ = ATTN_TILE
    nt = S // TQ
    qb, kb, vb = COL_Q // HEAD_DIM, COL_K // HEAD_DIM, COL_V // HEAD_DIM

    def cur(col0):
        return pl.BlockSpec((TQ, HEAD_DIM), lambda b, h, n: (b * nt + n, col0 + h))

    def prev(col0):
        return pl.BlockSpec((TQ, HEAD_DIM), lambda b, h, n: (b * nt + jnp.maximum(n - 1, 0), col0 + h))

    big = pltpu.VMEM((2 * TQ, HEAD_DIM), _F32)
    tile = pltpu.VMEM((TQ, HEAD_DIM), _F32)
    return pl.pallas_call(
        _attn_kernel,
        grid=(B, KV_HEADS, nt),
        in_specs=[cur(qb), cur(qb + KV_HEADS), cur(qb + 2 * KV_HEADS),
                  cur(kb), prev(kb), cur(vb), prev(vb)],
        out_specs=pl.BlockSpec((TQ, HEAD_DIM), lambda b, h, n: (b * nt + n, h)),
        out_shape=jax.ShapeDtypeStruct((B * S, WIDTH), _BF),
        scratch_shapes=[big, big, tile, tile, tile, tile, tile, tile],
        compiler_params=_params("parallel", "parallel", "parallel"),
        name="attn",
    )(proj, proj, proj, proj, proj, proj, proj)


def _sgu_kernel(z_ref, g_ref, ws_ref, b_ref, o_ref, *, tb):
    gz = jax.nn.gelu(z_ref[...])
    u = gz[:, :WIDTH]
    vv = _rms(gz[:, WIDTH:], g_ref[...]).astype(_BF)
    row = lax.broadcasted_iota(jnp.int32, (SG_CHUNK, SG_CHUNK), 0)
    col = lax.broadcasted_iota(jnp.int32, (SG_CHUNK, SG_CHUNK), 1)
    tri = row >= col
    for g in range(SG_GROUPS):
        w = jnp.where(tri, ws_ref[g], 0.0).astype(_BF)
        cs = slice(g * SG_CHUNK, (g + 1) * SG_CHUNK)
        for c in range(tb // SG_CHUNK):
            rs = slice(c * SG_CHUNK, (c + 1) * SG_CHUNK)
            mixed = _dot(w, vv[rs, cs]) + b_ref[g]
            o_ref[rs, cs] = (u[rs, cs] * mixed).astype(_BF)


def _sgu(proj, sg_norm, sg_w, sg_b, layer, T, tb):
    return pl.pallas_call(
        functools.partial(_sgu_kernel, tb=tb),
        grid=(T // tb,),
        in_specs=[
            pl.BlockSpec((tb, 2 * WIDTH), lambda i: (i, COL_ZD // (2 * WIDTH))),
            pl.BlockSpec((None, 1, WIDTH), lambda i: (layer, 0, 0)),
            pl.BlockSpec((None, SG_GROUPS, SG_CHUNK, SG_CHUNK), lambda i: (layer, 0, 0, 0)),
            pl.BlockSpec((None, SG_GROUPS, SG_CHUNK, 1), lambda i: (layer, 0, 0, 0)),
        ],
        out_specs=pl.BlockSpec((tb, WIDTH), lambda i: (i, 0)),
        out_shape=jax.ShapeDtypeStruct((T, WIDTH), _BF),
        compiler_params=_params("parallel"),
        name="sgu",
    )(proj, sg_norm, sg_w, sg_b)


def _merge_kernel(x_ref, g_ref, ya_ref, yb_ref, yc_ref, yd_ref, wg0_ref, wg1_ref, wg2_ref, wg3_ref,
                  bg_ref, wb_ref, wo_ref, o_ref, h_ref):
    @pl.when(pl.program_id(1) == 0)
    def _():
        x = x_ref[...]
        h_ref[...] = _rms(x, g_ref[...]).astype(_BF)
        o_ref[...] = x

    h = h_ref[...]
    merged = None
    for b, (y_ref, wg_ref) in enumerate(((ya_ref, wg0_ref), (yb_ref, wg1_ref),
                                         (yc_ref, wg2_ref), (yd_ref, wg3_ref))):
        gate = jax.nn.sigmoid(_dot(h, wg_ref[...]) + bg_ref[b:b + 1, :])
        term = gate * _dot(y_ref[...], wb_ref[b])
        merged = term if merged is None else merged + term
    o_ref[...] += _dot(merged.astype(_BF), wo_ref[...])


def _merge(x2, norm, ys, w_in, b_gate, w_branch, w_out, layer, tm, tn):
    T, D = x2.shape
    y_spec = pl.BlockSpec((tm, WIDTH), lambda i, j: (i, 0))

    def gate_spec(b):
        col0 = (COL_GATES + b * D) // tn
        return pl.BlockSpec((None, D, tn), lambda i, j: (layer, 0, col0 + j))

    return pl.pallas_call(
        _merge_kernel,
        grid=(T // tm, D // tn),
        in_specs=[
            pl.BlockSpec((tm, D), lambda i, j: (i, 0)),
            pl.BlockSpec((None, 1, D), lambda i, j: (layer, 0, 0)),
            y_spec, y_spec, y_spec, y_spec,
            gate_spec(0), gate_spec(1), gate_spec(2), gate_spec(3),
            pl.BlockSpec((None, N_BRANCH, tn), lambda i, j: (layer, 0, j)),
            pl.BlockSpec((None, N_BRANCH, WIDTH, tn), lambda i, j: (layer, 0, 0, j)),
            pl.BlockSpec((None, tn, D), lambda i, j: (layer, j, 0)),
        ],
        out_specs=pl.BlockSpec((tm, D), lambda i, j: (i, 0)),
        out_shape=jax.ShapeDtypeStruct((T, D), _F32),
        scratch_shapes=[pltpu.VMEM((tm, D), _BF)],
        compiler_params=_params("parallel", "arbitrary"),
        name="merge",
    )(x2, norm, *ys, w_in, w_in, w_in, w_in, b_gate, w_branch, w_out)


def _ple_kernel(x_ref, xj_ref, g_ref, p_ref, wg_ref, wp_ref, o_ref, h_ref):
    @pl.when(pl.program_id(1) == 0)
    def _():
        h_ref[...] = _rms(x_ref[...], g_ref[...]).astype(_BF)

    gate = jax.nn.sigmoid(_dot(h_ref[...], wg_ref[...]))
    o_ref[...] = xj_ref[...] + gate * _dot(p_ref[...].astype(_BF), wp_ref[...])


def _ple(x2, norm, p, w_gate, w_proj, layer, tm, tn):
    T, D = x2.shape
    return pl.pallas_call(
        _ple_kernel,
        grid=(T // tm, D // tn),
        in_specs=[
            pl.BlockSpec((tm, D), lambda i, j: (i, 0)),
            pl.BlockSpec((tm, tn), lambda i, j: (i, j)),
            pl.BlockSpec((None, 1, D), lambda i, j: (layer, 0, 0)),
            pl.BlockSpec((None, tm, PLE_DIM), lambda i, j: (layer, i, 0)),
            pl.BlockSpec((None, D, tn), lambda i, j: (layer, 0, j)),
            pl.BlockSpec((None, PLE_DIM, tn), lambda i, j: (layer, 0, j)),
        ],
        out_specs=pl.BlockSpec((tm, tn), lambda i, j: (i, j)),
        out_shape=jax.ShapeDtypeStruct((T, D), _F32),
        scratch_shapes=[pltpu.VMEM((tm, D), _BF)],
        compiler_params=_params("parallel", "arbitrary"),
        name="ple",
    )(x2, x2, norm, p, w_gate, w_proj)


def _block_diag(w):
    L = w.shape[0]
    per = RNN_BLOCK // RNN_HEAD_DIM
    w = w.reshape(L, RNN_HEADS // per, per, RNN_HEAD_DIM, RNN_HEAD_DIM)
    eye = jnp.eye(per, dtype=w.dtype)
    bd = jnp.einsum('lphij,hk->lphikj', w, eye)
    return bd.reshape(L, RNN_HEADS // per, RNN_BLOCK, RNN_BLOCK).astype(_BF)


def kernel(x, p, ffn1_norm, ffn1_w1, ffn1_w3, ffn1_w2, mix_norm, w_in, b_gate, conv_w, conv_b, rg_wa, rg_ba, rg_wx, rg_bx, rg_lambda, pool_w, pool_scale, q_gain, k_gain, sg_norm, sg_w, sg_b, w_branch, w_out, ffn2_norm, ffn2_w1, ffn2_w3, ffn2_w2, ple_norm, ple_gate_w, ple_proj):
    B, S, D = x.shape
    L = w_in.shape[0]
    T = B * S
    assert D == D_MODEL and S % ATTN_TILE == 0
    tm = 512

    def vec(a):
        return a.reshape(L, 1, a.shape[-1])

    def pad_ff(w, axis):
        pads = [(0, 0)] * 3
        pads[axis] = (0, D_FF_PAD - D_FF)
        return jnp.pad(w, pads).astype(_BF)

    f1 = (pad_ff(ffn1_w1, 2), pad_ff(ffn1_w3, 2), pad_ff(ffn1_w2, 1))
    f2 = (pad_ff(ffn2_w1, 2), pad_ff(ffn2_w3, 2), pad_ff(ffn2_w2, 1))
    w_in_b = w_in.astype(_BF)
    w_branch_b = w_branch.astype(_BF)
    w_out_b = w_out.astype(_BF)
    ple_gate_b = ple_gate_w.astype(_BF)
    ple_proj_b = ple_proj.astype(_BF)
    pool_w_b = pool_w.astype(_BF)
    wa_bd, wx_bd = _block_diag(rg_wa), _block_diag(rg_wx)
    ones = jnp.ones((L, COL_Q), _F32)
    qk_gain = jnp.concatenate(
        [ones, jnp.tile(q_gain, (1, (COL_K - COL_Q) // HEAD_DIM)),
         jnp.tile(k_gain, (1, (COL_V - COL_K) // HEAD_DIM)),
         jnp.ones((L, MIX_COLS - COL_V), _F32)], axis=1).reshape(L, 1, MIX_COLS)
    p2 = p.reshape(L, T, PLE_DIM)
    sg_b4 = sg_b.reshape(L, SG_GROUPS, SG_CHUNK, 1)

    x2 = x.reshape(T, D)
    for i in range(L):
        x2 = _ffn(x2, vec(ffn1_norm), *f1, i, tm)
        proj = _proj(x2, vec(mix_norm), w_in_b, qk_gain, i, tm, 512)
        ya = _rglru(proj, conv_w, vec(conv_b), wa_bd, vec(rg_ba), wx_bd, vec(rg_bx), vec(rg_lambda),
                    i, B, S, 256)
        yb = _pool(proj, pool_w_b, vec(pool_scale), i, B, S, 512)
        yc = _attn(proj, B, S)
        yd = _sgu(proj, vec(sg_norm), sg_w, sg_b4, i, T, 512)
        x2 = _merge(x2, vec(mix_norm), (ya, yb, yc, yd), w_in_b, b_gate, w_branch_b, w_out_b, i, tm, 256)
        x2 = _ffn(x2, vec(ffn2_norm), *f2, i, tm)
        x2 = _ple(x2, vec(ple_norm), p2, ple_gate_b, ple_proj_b, i, tm, 512)
    return x2.reshape(B, S, D)
```

```python
import functools

import jax
import jax.numpy as jnp
from jax import lax
from jax.experimental import pallas as pl
from jax.experimental.pallas import tpu as pltpu

EPS = 1e-6
NEG_INF = -1e30
D_MODEL = 2048
D_FF = 5504
LANES = 128
FF_TILE = 512
D_FF_PAD = -(-D_FF // FF_TILE) * FF_TILE
PLE_DIM = 256
WIDTH = 1024
RNN_HEADS = 16
RNN_HEAD_DIM = 64
RNN_BLOCK = 256
RG_C = 8.0
CONV_WIDTH = 4
POOL_WINDOWS = (2, 4, 8, 16)
POOL_GROUP = 256
POOL_HALO = 16
CONV_HALO = 8
HEAD_DIM = 128
KV_HEADS = 8
ATTN_DILATIONS = (1, 4, 16)
ATTN_BLOCK = 128
ATTN_TILE = ATTN_BLOCK * max(ATTN_DILATIONS)
ATTN_UNROLL = 8
SG_CHUNK = 128
SG_GROUPS = 8
N_BRANCH = 4
COL_XA, COL_GA, COL_XB, COL_Q, COL_K, COL_V, COL_ZD, COL_GATES = (
    0, 1024, 2048, 3072, 6144, 7168, 8192, 10240)
MIX_COLS = COL_GATES
VMEM_LIMIT = 56 * 1024 * 1024

_BF = jnp.bfloat16
_F32 = jnp.float32


def _params(*sem):
    return pltpu.CompilerParams(dimension_semantics=sem, vmem_limit_bytes=VMEM_LIMIT)


def _rms(x, g):
    return x * lax.rsqrt(jnp.mean(x * x, axis=-1, keepdims=True) + EPS) * g


def _dot(a, b):
    return jnp.dot(a, b, preferred_element_type=_F32)


def _ffn_kernel(x_ref, g_ref, w1_ref, w3_ref, w2_ref, o_ref, h_ref):
    @pl.when(pl.program_id(1) == 0)
    def _():
        x = x_ref[...]
        h_ref[...] = _rms(x, g_ref[...]).astype(_BF)
        o_ref[...] = x

    h = h_ref[...]
    a = _dot(h, w1_ref[...])
    b = _dot(h, w3_ref[...])
    act = (0.5 * (a * jax.nn.sigmoid(a)) * b).astype(_BF)
    for c in range(o_ref.shape[1] // FF_TILE):
        sl = slice(c * FF_TILE, (c + 1) * FF_TILE)
        o_ref[:, sl] += _dot(act, w2_ref[:, sl])


def _ffn(x2, norm, w1, w3, w2, layer, tm):
    T, D = x2.shape
    tf = FF_TILE
    return pl.pallas_call(
        _ffn_kernel,
        grid=(T // tm, w1.shape[2] // tf),
        in_specs=[
            pl.BlockSpec((tm, D), lambda i, j: (i, 0)),
            pl.BlockSpec((None, 1, D), lambda i, j: (layer, 0, 0)),
            pl.BlockSpec((None, D, tf), lambda i, j: (layer, 0, j)),
            pl.BlockSpec((None, D, tf), lambda i, j: (layer, 0, j)),
            pl.BlockSpec((None, tf, D), lambda i, j: (layer, j, 0)),
        ],
        out_specs=pl.BlockSpec((tm, D), lambda i, j: (i, 0)),
        out_shape=jax.ShapeDtypeStruct((T, D), _F32),
        scratch_shapes=[pltpu.VMEM((tm, D), _BF)],
        compiler_params=_params("parallel", "arbitrary"),
        name="ffn",
    )(x2, norm, w1, w3, w2)


def _proj_kernel(x_ref, g_ref, w_ref, gain_ref, o_ref, h_ref, *, norm_lo, norm_hi):
    j = pl.program_id(1)

    @pl.when(j == 0)
    def _():
        h_ref[...] = _rms(x_ref[...], g_ref[...]).astype(_BF)

    acc = _dot(h_ref[...], w_ref[...])
    is_qk = jnp.logical_and(j >= norm_lo, j < norm_hi)

    @pl.when(is_qk)
    def _():
        for c in range(acc.shape[1] // HEAD_DIM):
            sl = slice(c * HEAD_DIM, (c + 1) * HEAD_DIM)
            o_ref[:, sl] = _rms(acc[:, sl], gain_ref[:, sl])

    @pl.when(jnp.logical_not(is_qk))
    def _():
        o_ref[...] = acc


def _proj(x2, norm, w_in, gain, layer, tm, tn):
    T, D = x2.shape
    return pl.pallas_call(
        functools.partial(_proj_kernel, norm_lo=COL_Q // tn, norm_hi=COL_V // tn),
        grid=(T // tm, MIX_COLS // tn),
        in_specs=[
            pl.BlockSpec((tm, D), lambda i, j: (i, 0)),
            pl.BlockSpec((None, 1, D), lambda i, j: (layer, 0, 0)),
            pl.BlockSpec((None, D, tn), lambda i, j: (layer, 0, j)),
            pl.BlockSpec((None, 1, tn), lambda i, j: (layer, 0, j)),
        ],
        out_specs=pl.BlockSpec((tm, tn), lambda i, j: (i, j)),
        out_shape=jax.ShapeDtypeStruct((T, MIX_COLS), _F32),
        scratch_shapes=[pltpu.VMEM((tm, D), _BF)],
        compiler_params=_params("parallel", "arbitrary"),
        name="proj",
    )(x2, norm, w_in, gain)


def _rglru_kernel(xa_ref, halo_ref, ga_ref, cw_ref, cb_ref, wa_ref, ba_ref, wx_ref, bx_ref,
                  lam_ref, o_ref, carry_ref, *, tb):
    i = pl.program_id(1)

    @pl.when(i == 0)
    def _():
        carry_ref[...] = jnp.zeros_like(carry_ref)

    x = xa_ref[...]
    halo = jnp.where(i == 0, 0.0, halo_ref[...])
    xe = jnp.concatenate([halo, x], axis=0)
    cw = cw_ref[...]
    y = cb_ref[...] + cw[0:1] * xe
    for j in range(1, CONV_WIDTH):
        y = y + cw[j:j + 1] * pltpu.roll(xe, j, axis=0)
    xc = y[CONV_HALO:]
    xcb = xc.astype(_BF)
    r_lin, i_lin = [], []
    for p in range(WIDTH // RNN_BLOCK):
        sl = slice(p * RNN_BLOCK, (p + 1) * RNN_BLOCK)
        r_lin.append(_dot(xcb[:, sl], wa_ref[p]))
        i_lin.append(_dot(xcb[:, sl], wx_ref[p]))
    r = jax.nn.sigmoid(jnp.concatenate(r_lin, axis=1) + ba_ref[...])
    ig = jax.nn.sigmoid(jnp.concatenate(i_lin, axis=1) + bx_ref[...])
    z = -lam_ref[...]
    softplus = jnp.maximum(z, 0.0) + jnp.log1p(jnp.exp(-jnp.abs(z)))
    log_a = (-RG_C * r) * softplus
    a = jnp.exp(log_a)
    b = jnp.sqrt(1.0 - a * a) * (ig * xc)
    row = lax.broadcasted_iota(jnp.int32, (tb, 1), 0)
    s = 1
    while s < tb:
        keep = row >= s
        a_prev = jnp.where(keep, pltpu.roll(a, s, axis=0), 1.0)
        b_prev = jnp.where(keep, pltpu.roll(b, s, axis=0), 0.0)
        b = a * b_prev + b
        a = a * a_prev
        s *= 2
    h = a * carry_ref[0:1, :] + b
    carry_ref[0:1, :] = h[tb - 1:tb, :]
    o_ref[...] = (h * jax.nn.gelu(ga_ref[...])).astype(_BF)


def _rglru(proj, cw, cb, wa, ba, wx, bx, lam, layer, B, S, tb):
    nb = S // tb
    hb = tb // CONV_HALO
    vec = pl.BlockSpec((None, 1, WIDTH), lambda b, i: (layer, 0, 0))
    mat = pl.BlockSpec((None, WIDTH // RNN_BLOCK, RNN_BLOCK, RNN_BLOCK), lambda b, i: (layer, 0, 0, 0))
    return pl.pallas_call(
        functools.partial(_rglru_kernel, tb=tb),
        grid=(B, nb),
        in_specs=[
            pl.BlockSpec((tb, WIDTH), lambda b, i: (b * nb + i, COL_XA // WIDTH)),
            pl.BlockSpec((CONV_HALO, WIDTH),
                         lambda b, i: (jnp.maximum((b * nb + i) * hb - 1, 0), COL_XA // WIDTH)),
            pl.BlockSpec((tb, WIDTH), lambda b, i: (b * nb + i, COL_GA // WIDTH)),
            pl.BlockSpec((None, CONV_WIDTH, WIDTH), lambda b, i: (layer, 0, 0)),
            vec, mat, vec, mat, vec, vec,
        ],
        out_specs=pl.BlockSpec((tb, WIDTH), lambda b, i: (b * nb + i, 0)),
        out_shape=jax.ShapeDtypeStruct((B * S, WIDTH), _BF),
        scratch_shapes=[pltpu.VMEM((8, WIDTH), _F32)],
        compiler_params=_params("parallel", "arbitrary"),
        name="rglru",
    )(proj, proj, proj, cw, cb, wa, ba, wx, bx, lam)


def _pool_kernel(xb_ref, halo_ref, pw_ref, sc_ref, o_ref, *, tb):
    i = pl.program_id(1)
    x = xb_ref[...]
    halo = jnp.where(i == 0, 0.0, halo_ref[...])
    xe = jnp.concatenate([halo, x], axis=0)
    pos = i * tb + lax.broadcasted_iota(jnp.int32, (tb, 1), 0)
    for g, win in enumerate(POOL_WINDOWS):
        sl = slice(g * POOL_GROUP, (g + 1) * POOL_GROUP)
        s = xe[:, sl]
        sh = 1
        while sh < win:
            s = s + pltpu.roll(s, sh, axis=0)
            sh *= 2
        cnt = jnp.minimum(pos + 1, win).astype(_F32)
        pooled = s[POOL_HALO:] / cnt - x[:, sl]
        y = _dot(pooled.astype(_BF), pw_ref[g])
        o_ref[:, sl] = (y * sc_ref[:, sl]).astype(_BF)


def _pool(proj, pw, sc, layer, B, S, tb):
    nb = S // tb
    hb = tb // POOL_HALO
    return pl.pallas_call(
        functools.partial(_pool_kernel, tb=tb),
        grid=(B, nb),
        in_specs=[
            pl.BlockSpec((tb, WIDTH), lambda b, i: (b * nb + i, COL_XB // WIDTH)),
            pl.BlockSpec((POOL_HALO, WIDTH),
                         lambda b, i: (jnp.maximum((b * nb + i) * hb - 1, 0), COL_XB // WIDTH)),
            pl.BlockSpec((None, len(POOL_WINDOWS), POOL_GROUP, POOL_GROUP), lambda b, i: (layer, 0, 0, 0)),
            pl.BlockSpec((None, 1, WIDTH), lambda b, i: (layer, 0, 0)),
        ],
        out_specs=pl.BlockSpec((tb, WIDTH), lambda b, i: (b * nb + i, 0)),
        out_shape=jax.ShapeDtypeStruct((B * S, WIDTH), _BF),
        compiler_params=_params("parallel", "parallel"),
        name="pool",
    )(proj, proj, pw, sc)


def _attn_kernel(q0_ref, q1_ref, q2_ref, kc_ref, kp_ref, vc_ref, vp_ref, o_ref,
                 ks_ref, vs_ref, o0_ref, o1_ref, o2_ref, l0_ref, l1_ref, l2_ref, bias_ref):
    n = pl.program_id(2)
    TQ = ATTN_TILE
    ks_ref[TQ:, :] = kc_ref[...]
    vs_ref[TQ:, :] = vc_ref[...]

    @pl.when(n == 0)
    def _():
        ks_ref[:TQ, :] = jnp.zeros((TQ, HEAD_DIM), _F32)
        vs_ref[:TQ, :] = jnp.zeros((TQ, HEAD_DIM), _F32)

    @pl.when(n > 0)
    def _():
        ks_ref[:TQ, :] = kp_ref[...]
        vs_ref[:TQ, :] = vp_ref[...]

    row = lax.broadcasted_iota(jnp.int32, (ATTN_BLOCK, 2 * ATTN_BLOCK), 0)
    col = lax.broadcasted_iota(jnp.int32, (ATTN_BLOCK, 2 * ATTN_BLOCK), 1)
    band = jnp.logical_and(col >= row, col <= row + ATTN_BLOCK)
    bias_ref[0] = jnp.where(band, 0.0, NEG_INF)
    bias_ref[1] = jnp.where(jnp.logical_and(band, col >= ATTN_BLOCK), 0.0, NEG_INF)
    scale = HEAD_DIM ** -0.5
    exp2_scale = scale * 1.4426950408889634

    for d, q_ref, og_ref, lg_ref in ((ATTN_DILATIONS[0], q0_ref, o0_ref, l0_ref),
                                     (ATTN_DILATIONS[1], q1_ref, o1_ref, l1_ref),
                                     (ATTN_DILATIONS[2], q2_ref, o2_ref, l2_ref)):
        stride = None if d == 1 else d

        def body(idx, carry, d=d, q_ref=q_ref, og_ref=og_ref, lg_ref=lg_ref, stride=stride):
            r = idx % d
            m = idx // d
            qs = m * (ATTN_BLOCK * d) + r
            kst = TQ + qs - ATTN_BLOCK * d
            q = q_ref[pl.ds(qs, ATTN_BLOCK, stride=stride), :].astype(_BF)
            k = ks_ref[pl.ds(kst, 2 * ATTN_BLOCK, stride=stride), :].astype(_BF)
            v = vs_ref[pl.ds(kst, 2 * ATTN_BLOCK, stride=stride), :].astype(_BF)
            first = jnp.logical_and(n == 0, m == 0).astype(jnp.int32)
            s = lax.dot_general(q, k, (((1,), (1,)), ((), ())), preferred_element_type=_F32) + bias_ref[first]
            mx = jnp.max(s, axis=-1, keepdims=True)
            p = jnp.exp2((s - mx) * exp2_scale)
            l = jnp.sum(p, axis=-1, keepdims=True)
            o = _dot(p.astype(_BF), v) / l
            lse = mx * scale + jnp.log(l)
            og_ref[pl.ds(qs, ATTN_BLOCK, stride=stride), :] = o
            lg_ref[pl.ds(qs, ATTN_BLOCK, stride=stride), :] = jnp.broadcast_to(lse, (ATTN_BLOCK, HEAD_DIM))
            return carry

        lax.fori_loop(0, TQ // ATTN_BLOCK, body, 0, unroll=ATTN_UNROLL)

    l0, l1, l2 = l0_ref[...], l1_ref[...], l2_ref[...]
    mx = jnp.maximum(jnp.maximum(l0, l1), l2)
    w0, w1, w2 = jnp.exp(l0 - mx), jnp.exp(l1 - mx), jnp.exp(l2 - mx)
    out = (w0 * o0_ref[...] + w1 * o1_ref[...] + w2 * o2_ref[...]) / (w0 + w1 + w2)
    o_ref[...] = out.astype(_BF)


def _attn(proj, B, S):
    TQ = ATTN_TILE
    nt = S // TQ
    qb, kb, vb = COL_Q // HEAD_DIM, COL_K // HEAD_DIM, COL_V // HEAD_DIM

    def cur(col0):
        return pl.BlockSpec((TQ, HEAD_DIM), lambda b, h, n: (b * nt + n, col0 + h))

    def prev(col0):
        return pl.BlockSpec((TQ, HEAD_DIM), lambda b, h, n: (b * nt + jnp.maximum(n - 1, 0), col0 + h))

    big = pltpu.VMEM((2 * TQ, HEAD_DIM), _F32)
    tile = pltpu.VMEM((TQ, HEAD_DIM), _F32)
    return pl.pallas_call(
        _attn_kernel,
        grid=(B, KV_HEADS, nt),
        in_specs=[cur(qb), cur(qb + KV_HEADS), cur(qb + 2 * KV_HEADS),
                  cur(kb), prev(kb), cur(vb), prev(vb)],
        out_specs=pl.BlockSpec((TQ, HEAD_DIM), lambda b, h, n: (b * nt + n, h)),
        out_shape=jax.ShapeDtypeStruct((B * S, WIDTH), _BF),
        scratch_shapes=[big, big, tile, tile, tile, tile, tile, tile,
                        pltpu.VMEM((2, ATTN_BLOCK, 2 * ATTN_BLOCK), _F32)],
        compiler_params=_params("parallel", "parallel", "parallel"),
        name="attn",
    )(proj, proj, proj, proj, proj, proj, proj)


def _sgu_kernel(z_ref, g_ref, ws_ref, b_ref, o_ref, *, tb):
    gz = jax.nn.gelu(z_ref[...])
    u = gz[:, :WIDTH]
    vv = _rms(gz[:, WIDTH:], g_ref[...]).astype(_BF)
    row = lax.broadcasted_iota(jnp.int32, (SG_CHUNK, SG_CHUNK), 0)
    col = lax.broadcasted_iota(jnp.int32, (SG_CHUNK, SG_CHUNK), 1)
    tri = row >= col
    for g in range(SG_GROUPS):
        w = jnp.where(tri, ws_ref[g], 0.0).astype(_BF)
        cs = slice(g * SG_CHUNK, (g + 1) * SG_CHUNK)
        for c in range(tb // SG_CHUNK):
            rs = slice(c * SG_CHUNK, (c + 1) * SG_CHUNK)
            mixed = _dot(w, vv[rs, cs]) + b_ref[g]
            o_ref[rs, cs] = (u[rs, cs] * mixed).astype(_BF)


def _sgu(proj, sg_norm, sg_w, sg_b, layer, T, tb):
    return pl.pallas_call(
        functools.partial(_sgu_kernel, tb=tb),
        grid=(T // tb,),
        in_specs=[
            pl.BlockSpec((tb, 2 * WIDTH), lambda i: (i, COL_ZD // (2 * WIDTH))),
            pl.BlockSpec((None, 1, WIDTH), lambda i: (layer, 0, 0)),
            pl.BlockSpec((None, SG_GROUPS, SG_CHUNK, SG_CHUNK), lambda i: (layer, 0, 0, 0)),
            pl.BlockSpec((None, SG_GROUPS, SG_CHUNK, 1), lambda i: (layer, 0, 0, 0)),
        ],
        out_specs=pl.BlockSpec((tb, WIDTH), lambda i: (i, 0)),
        out_shape=jax.ShapeDtypeStruct((T, WIDTH), _BF),
        compiler_params=_params("parallel"),
        name="sgu",
    )(proj, sg_norm, sg_w, sg_b)


def _merge_kernel(x_ref, g_ref, ya_ref, yb_ref, yc_ref, yd_ref, wg0_ref, wg1_ref, wg2_ref, wg3_ref,
                  bg_ref, wb_ref, wo_ref, o_ref, h_ref):
    @pl.when(pl.program_id(1) == 0)
    def _():
        x = x_ref[...]
        h_ref[...] = _rms(x, g_ref[...]).astype(_BF)
        o_ref[...] = x

    h = h_ref[...]
    merged = None
    for b, (y_ref, wg_ref) in enumerate(((ya_ref, wg0_ref), (yb_ref, wg1_ref),
                                         (yc_ref, wg2_ref), (yd_ref, wg3_ref))):
        gate = jax.nn.sigmoid(_dot(h, wg_ref[...]) + bg_ref[b:b + 1, :])
        term = gate * _dot(y_ref[...], wb_ref[b])
        merged = term if merged is None else merged + term
    o_ref[...] += _dot(merged.astype(_BF), wo_ref[...])


def _merge(x2, norm, ys, w_in, b_gate, w_branch, w_out, layer, tm, tn):
    T, D = x2.shape
    y_spec = pl.BlockSpec((tm, WIDTH), lambda i, j: (i, 0))

    def gate_spec(b):
        col0 = (COL_GATES + b * D) // tn
        return pl.BlockSpec((None, D, tn), lambda i, j: (layer, 0, col0 + j))

    return pl.pallas_call(
        _merge_kernel,
        grid=(T // tm, D // tn),
        in_specs=[
            pl.BlockSpec((tm, D), lambda i, j: (i, 0)),
            pl.BlockSpec((None, 1, D), lambda i, j: (layer, 0, 0)),
            y_spec, y_spec, y_spec, y_spec,
            gate_spec(0), gate_spec(1), gate_spec(2), gate_spec(3),
            pl.BlockSpec((None, N_BRANCH, tn), lambda i, j: (layer, 0, j)),
            pl.BlockSpec((None, N_BRANCH, WIDTH, tn), lambda i, j: (layer, 0, 0, j)),
            pl.BlockSpec((None, tn, D), lambda i, j: (layer, j, 0)),
        ],
        out_specs=pl.BlockSpec((tm, D), lambda i, j: (i, 0)),
        out_shape=jax.ShapeDtypeStruct((T, D), _F32),
        scratch_shapes=[pltpu.VMEM((tm, D), _BF)],
        compiler_params=_params("parallel", "arbitrary"),
        name="merge",
    )(x2, norm, *ys, w_in, w_in, w_in, w_in, b_gate, w_branch, w_out)


def _ple_kernel(x_ref, xj_ref, g_ref, p_ref, wg_ref, wp_ref, o_ref, h_ref):
    @pl.when(pl.program_id(1) == 0)
    def _():
        h_ref[...] = _rms(x_ref[...], g_ref[...]).astype(_BF)

    gate = jax.nn.sigmoid(_dot(h_ref[...], wg_ref[...]))
    o_ref[...] = xj_ref[...] + gate * _dot(p_ref[...].astype(_BF), wp_ref[...])


def _ple(x2, norm, p, w_gate, w_proj, layer, tm, tn):
    T, D = x2.shape
    return pl.pallas_call(
        _ple_kernel,
        grid=(T // tm, D // tn),
        in_specs=[
            pl.BlockSpec((tm, D), lambda i, j: (i, 0)),
            pl.BlockSpec((tm, tn), lambda i, j: (i, j)),
            pl.BlockSpec((None, 1, D), lambda i, j: (layer, 0, 0)),
            pl.BlockSpec((None, tm, PLE_DIM), lambda i, j: (layer, i, 0)),
            pl.BlockSpec((None, D, tn), lambda i, j: (layer, 0, j)),
            pl.BlockSpec((None, PLE_DIM, tn), lambda i, j: (layer, 0, j)),
        ],
        out_specs=pl.BlockSpec((tm, tn), lambda i, j: (i, j)),
        out_shape=jax.ShapeDtypeStruct((T, D), _F32),
        scratch_shapes=[pltpu.VMEM((tm, D), _BF)],
        compiler_params=_params("parallel", "arbitrary"),
        name="ple",
    )(x2, x2, norm, p, w_gate, w_proj)


def _block_diag(w):
    L = w.shape[0]
    per = RNN_BLOCK // RNN_HEAD_DIM
    w = w.reshape(L, RNN_HEADS // per, per, RNN_HEAD_DIM, RNN_HEAD_DIM)
    eye = jnp.eye(per, dtype=w.dtype)
    bd = jnp.einsum('lphij,hk->lphikj', w, eye)
    return bd.reshape(L, RNN_HEADS // per, RNN_BLOCK, RNN_BLOCK).astype(_BF)


def kernel(x, p, ffn1_norm, ffn1_w1, ffn1_w3, ffn1_w2, mix_norm, w_in, b_gate, conv_w, conv_b, rg_wa, rg_ba, rg_wx, rg_bx, rg_lambda, pool_w, pool_scale, q_gain, k_gain, sg_norm, sg_w, sg_b, w_branch, w_out, ffn2_norm, ffn2_w1, ffn2_w3, ffn2_w2, ple_norm, ple_gate_w, ple_proj):
    B, S, D = x.shape
    L = w_in.shape[0]
    T = B * S
    assert D == D_MODEL and S % ATTN_TILE == 0
    tm = 512

    def vec(a):
        return a.reshape(L, 1, a.shape[-1])

    def pad_ff(w, axis):
        pads = [(0, 0)] * 3
        pads[axis] = (0, D_FF_PAD - D_FF)
        return jnp.pad(w, pads).astype(_BF)

    f1 = (pad_ff(ffn1_w1, 2), pad_ff(ffn1_w3, 2), pad_ff(ffn1_w2, 1))
    f2 = (pad_ff(ffn2_w1, 2), pad_ff(ffn2_w3, 2), pad_ff(ffn2_w2, 1))
    w_in_b = w_in.astype(_BF)
    w_branch_b = w_branch.astype(_BF)
    w_out_b = w_out.astype(_BF)
    ple_gate_b = ple_gate_w.astype(_BF)
    ple_proj_b = ple_proj.astype(_BF)
    pool_w_b = pool_w.astype(_BF)
    wa_bd, wx_bd = _block_diag(rg_wa), _block_diag(rg_wx)
    ones = jnp.ones((L, COL_Q), _F32)
    qk_gain = jnp.concatenate(
        [ones, jnp.tile(q_gain, (1, (COL_K - COL_Q) // HEAD_DIM)),
         jnp.tile(k_gain, (1, (COL_V - COL_K) // HEAD_DIM)),
         jnp.ones((L, MIX_COLS - COL_V), _F32)], axis=1).reshape(L, 1, MIX_COLS)
    p2 = p.reshape(L, T, PLE_DIM)
    sg_b4 = sg_b.reshape(L, SG_GROUPS, SG_CHUNK, 1)

    x2 = x.reshape(T, D)
    for i in range(L):
        x2 = _ffn(x2, vec(ffn1_norm), *f1, i, 1024)
        proj = _proj(x2, vec(mix_norm), w_in_b, qk_gain, i, 1024, 512)
        ya = _rglru(proj, conv_w, vec(conv_b), wa_bd, vec(rg_ba), wx_bd, vec(rg_bx), vec(rg_lambda),
                    i, B, S, 256)
        yb = _pool(proj, pool_w_b, vec(pool_scale), i, B, S, 512)
        yc = _attn(proj, B, S)
        yd = _sgu(proj, vec(sg_norm), sg_w, sg_b4, i, T, 512)
        x2 = _merge(x2, vec(mix_norm), (ya, yb, yc, yd), w_in_b, b_gate, w_branch_b, w_out_b, i, tm, 256)
        x2 = _ffn(x2, vec(ffn2_norm), *f2, i, 1024)
        x2 = _ple(x2, vec(ple_norm), p2, ple_gate_b, ple_proj_b, i, 1024, 512)
    return x2.reshape(B, S, D)
```

```python
import functools

import numpy as np

import jax
import jax.numpy as jnp
from jax import lax
from jax.experimental import pallas as pl
from jax.experimental.pallas import tpu as pltpu

EPS = 1e-6
NEG_INF = -1e30
D_MODEL = 2048
D_FF = 5504
FF_TILE = 512
PLE_TILE = 512
PLE_DIM = 256
WIDTH = 1024
RNN_HEADS = 16
RNN_HEAD_DIM = 64
RNN_BLOCK = 256
RG_C = 8.0
CONV_WIDTH = 4
POOL_WINDOWS = (2, 4, 8, 16)
POOL_GROUP = 256
POOL_HALO = 16
CONV_HALO = 8
SCAN_GROUP = 8
HEAD_DIM = 128
KV_HEADS = 8
ATTN_DILATIONS = (1, 4, 16)
ATTN_BLOCK = 128
ATTN_TILE = ATTN_BLOCK * max(ATTN_DILATIONS)
ATTN_UNROLL = 16
SG_CHUNK = 128
SG_GROUPS = 8
N_BRANCH = 4
COL_XA, COL_GA, COL_XB, COL_Q, COL_K, COL_V, COL_ZD, COL_GATES = (
    0, 1024, 2048, 3072, 6144, 7168, 8192, 10240)
MIX_COLS = COL_GATES
NAT_ZD, NAT_XA, NAT_GA, NAT_XB, NAT_COLS = 0, 2048, 3072, 4096, 5120
QKV_Q, QKV_K, QKV_V, QKV_COLS = 0, 3072, 4096, 5120
ATTN_SLABS = 16
PERM_ROWS = 256
VMEM_LIMIT = 56 * 1024 * 1024

_BF = jnp.bfloat16
_F32 = jnp.float32


def _params(*sem):
    return pltpu.CompilerParams(dimension_semantics=sem, vmem_limit_bytes=VMEM_LIMIT)


def _rms(x, g):
    return x * lax.rsqrt(jnp.mean(x * x, axis=-1, keepdims=True) + EPS) * g


def _dot(a, b):
    return jnp.dot(a, b, preferred_element_type=_F32)


def _ffn_kernel(x_ref, g_ref, w1_ref, w3_ref, w2_ref, *rest, nf, ple_tiles):
    if ple_tiles:
        pg_ref, p_ref, wg_ref, wp_ref, o_ref, h_ref = rest
    else:
        o_ref, h_ref = rest
    j = pl.program_id(1)
    tf = w1_ref.shape[1]

    @pl.when(j == 0)
    def _():
        x = x_ref[...]
        h_ref[...] = _rms(x, g_ref[...]).astype(_BF)
        o_ref[...] = x

    def ffn_step(valid):
        h = h_ref[...]
        a = _dot(h, w1_ref[:, :valid])
        b = _dot(h, w3_ref[:, :valid])
        act = (0.5 * (a * jax.nn.sigmoid(a)) * b).astype(_BF)
        for c in range(o_ref.shape[1] // FF_TILE):
            sl = slice(c * FF_TILE, (c + 1) * FF_TILE)
            o_ref[:, sl] += _dot(act, w2_ref[:valid, sl])

    last_valid = D_FF - (nf - 1) * tf
    if last_valid == tf:
        pl.when(j < nf)(lambda: ffn_step(tf))
    else:
        pl.when(j < nf - 1)(lambda: ffn_step(tf))
        pl.when(j == nf - 1)(lambda: ffn_step(last_valid))

    if ple_tiles:
        @pl.when(j == nf)
        def _():
            h_ref[...] = _rms(o_ref[...], pg_ref[...]).astype(_BF)

        tn = wg_ref.shape[1]
        for c in range(ple_tiles):
            @pl.when(j == nf + c)
            def _(c=c):
                gate = jax.nn.sigmoid(_dot(h_ref[...], wg_ref[...]))
                sl = slice(c * tn, (c + 1) * tn)
                o_ref[:, sl] += gate * _dot(p_ref[...].astype(_BF), wp_ref[...])


def _ffn(x2, norm, w1, w3, w2, layer, tm, ple=None):
    T, D = x2.shape
    tf = FF_TILE
    nf = -(-D_FF // tf)
    last = nf - 1
    in_specs = [
        pl.BlockSpec((tm, D), lambda i, j: (i, 0)),
        pl.BlockSpec((None, 1, D), lambda i, j: (layer, 0, 0)),
        pl.BlockSpec((None, D, tf), lambda i, j: (layer, 0, jnp.minimum(j, last))),
        pl.BlockSpec((None, D, tf), lambda i, j: (layer, 0, jnp.minimum(j, last))),
        pl.BlockSpec((None, tf, D), lambda i, j: (layer, jnp.minimum(j, last), 0)),
    ]
    args = [x2, norm, w1, w3, w2]
    ple_tiles = 0
    if ple is not None:
        ple_norm, p, w_gate, w_proj = ple
        tn = PLE_TILE
        ple_tiles = D // tn
        in_specs += [
            pl.BlockSpec((None, 1, D), lambda i, j: (layer, 0, 0)),
            pl.BlockSpec((None, tm, PLE_DIM), lambda i, j: (layer, i, 0)),
            pl.BlockSpec((None, D, tn), lambda i, j: (layer, 0, jnp.maximum(j - nf, 0))),
            pl.BlockSpec((None, PLE_DIM, tn), lambda i, j: (layer, 0, jnp.maximum(j - nf, 0))),
        ]
        args += [ple_norm, p, w_gate, w_proj]
    return pl.pallas_call(
        functools.partial(_ffn_kernel, nf=nf, ple_tiles=ple_tiles),
        grid=(T // tm, nf + ple_tiles),
        in_specs=in_specs,
        out_specs=pl.BlockSpec((tm, D), lambda i, j: (i, 0)),
        out_shape=jax.ShapeDtypeStruct((T, D), _F32),
        scratch_shapes=[pltpu.VMEM((tm, D), _BF)],
        compiler_params=_params("parallel", "arbitrary"),
        name="ffn_ple" if ple_tiles else "ffn",
    )(*args)


def _slab_of_residue(r):
    return (r % 4) * 4 + r // 4


def _proj_kernel(x_ref, g_ref, w_ref, gain_ref, nat_ref, qkv_ref, h_ref, hp_ref, *, n_nat, n_norm):
    j = pl.program_id(1)
    tm = x_ref.shape[0]
    nsub = tm // PERM_ROWS

    @pl.when(j == 0)
    def _():
        h = _rms(x_ref[...], g_ref[...]).astype(_BF)
        h_ref[...] = h
        lam = lax.broadcasted_iota(jnp.int32, (PERM_ROWS, PERM_ROWS), 0)
        tau = lax.broadcasted_iota(jnp.int32, (PERM_ROWS, PERM_ROWS), 1)
        r = tau % ATTN_SLABS
        perm = (lam == _slab_of_residue(r) * (PERM_ROWS // ATTN_SLABS) + tau // ATTN_SLABS).astype(_BF)
        for s in range(nsub):
            rows = slice(s * PERM_ROWS, (s + 1) * PERM_ROWS)
            hp_ref[rows, :] = _dot(perm, h[rows, :]).astype(_BF)

    @pl.when(j < n_nat)
    def _():
        nat_ref[...] = _dot(h_ref[...], w_ref[...])

    def store_qkv(cols, y):
        run = PERM_ROWS // ATTN_SLABS
        for s in range(nsub):
            for slab in range(ATTN_SLABS):
                r0 = s * PERM_ROWS + slab * run
                qkv_ref[slab, s * run:(s + 1) * run, cols] = y[r0:r0 + run, :]

    is_qk = jnp.logical_and(j >= n_nat, j < n_nat + n_norm)

    @pl.when(is_qk)
    def _():
        acc = _dot(hp_ref[...], w_ref[...])
        for c in range(acc.shape[1] // HEAD_DIM):
            sl = slice(c * HEAD_DIM, (c + 1) * HEAD_DIM)
            store_qkv(sl, _rms(acc[:, sl], gain_ref[:, sl]))

    @pl.when(j >= n_nat + n_norm)
    def _():
        store_qkv(slice(None), _dot(hp_ref[...], w_ref[...]))


def _proj(x2, norm, w_in, gain, layer, tm, tn):
    T, D = x2.shape
    n_zd = (COL_GATES - COL_ZD) // tn
    n_nat = n_zd + COL_Q // tn
    n_norm = (COL_V - COL_Q) // tn
    n_all = MIX_COLS // tn
    assert tm % PERM_ROWS == 0 and ATTN_TILE % tm == 0
    per_tile = ATTN_TILE // tm

    def w_col(j):
        return jnp.where(j < n_zd, j + COL_ZD // tn, j - n_zd)

    return pl.pallas_call(
        functools.partial(_proj_kernel, n_nat=n_nat, n_norm=n_norm),
        grid=(T // tm, n_all),
        in_specs=[
            pl.BlockSpec((tm, D), lambda i, j: (i, 0)),
            pl.BlockSpec((None, 1, D), lambda i, j: (layer, 0, 0)),
            pl.BlockSpec((None, D, tn), lambda i, j: (layer, 0, w_col(j))),
            pl.BlockSpec((None, 1, tn), lambda i, j: (layer, 0, w_col(j))),
        ],
        out_specs=[
            pl.BlockSpec((tm, tn), lambda i, j: (i, jnp.minimum(j, n_nat - 1))),
            pl.BlockSpec((None, ATTN_SLABS, tm // ATTN_SLABS, tn),
                         lambda i, j: (i // per_tile, 0, i % per_tile, jnp.maximum(j - n_nat, 0))),
        ],
        out_shape=[jax.ShapeDtypeStruct((T, NAT_COLS), _F32),
                   jax.ShapeDtypeStruct((T // ATTN_TILE, ATTN_SLABS, ATTN_BLOCK, QKV_COLS), _F32)],
        scratch_shapes=[pltpu.VMEM((tm, D), _BF), pltpu.VMEM((tm, D), _BF)],
        compiler_params=_params("parallel", "arbitrary"),
        name="proj",
    )(x2, norm, w_in, gain)


def _rglru_kernel(xa_ref, halo_ref, ga_ref, cw_ref, cb_ref, wa_ref, ba_ref, wx_ref, bx_ref,
                  lam_ref, o_ref, carry_ref, sa_ref, sb_ref, xe_ref, *, tb):
    i = pl.program_id(1)

    @pl.when(i == 0)
    def _():
        carry_ref[...] = jnp.zeros_like(carry_ref)

    xe_ref[:CONV_HALO, :] = jnp.where(i == 0, 0.0, halo_ref[...])
    xe_ref[CONV_HALO:, :] = xa_ref[...]
    cw = cw_ref[...]
    xc = cb_ref[...] + cw[0:1] * xa_ref[...]
    for j in range(1, CONV_WIDTH):
        xc = xc + cw[j:j + 1] * xe_ref[pl.ds(CONV_HALO - j, tb), :]
    xcb = xc.astype(_BF)
    r_lin, i_lin = [], []
    for p in range(WIDTH // RNN_BLOCK):
        sl = slice(p * RNN_BLOCK, (p + 1) * RNN_BLOCK)
        r_lin.append(_dot(xcb[:, sl], wa_ref[p]))
        i_lin.append(_dot(xcb[:, sl], wx_ref[p]))
    r = jax.nn.sigmoid(jnp.concatenate(r_lin, axis=1) + ba_ref[...])
    ig = jax.nn.sigmoid(jnp.concatenate(i_lin, axis=1) + bx_ref[...])
    z = -lam_ref[...]
    softplus = jnp.maximum(z, 0.0) + jnp.log1p(jnp.exp(-jnp.abs(z)))
    log_a = (-RG_C * r) * softplus
    a = jnp.exp(log_a)
    y = 1.0 - a * a
    b = jnp.where(y > 0.0, y * lax.rsqrt(y), 0.0) * (ig * xc)
    ng = tb // SCAN_GROUP
    a = a.reshape(ng, SCAN_GROUP, WIDTH)
    b = b.reshape(ng, SCAN_GROUP, WIDTH)
    row = lax.broadcasted_iota(jnp.int32, (1, SCAN_GROUP, 1), 1)
    s = 1
    while s < SCAN_GROUP:
        keep = row >= s
        a_prev = jnp.where(keep, pltpu.roll(a, s, axis=1), 1.0)
        b_prev = jnp.where(keep, pltpu.roll(b, s, axis=1), 0.0)
        b = a * b_prev + b
        a = a * a_prev
        s *= 2
    sa_ref[...] = a
    sb_ref[...] = b

    def chain(g, carry):
        hg = sa_ref[g] * carry + sb_ref[g]
        sb_ref[g] = hg
        return hg[SCAN_GROUP - 1:SCAN_GROUP, :]

    carry_ref[0:1, :] = lax.fori_loop(0, ng, chain, carry_ref[0:1, :], unroll=8)
    h = sb_ref[...].reshape(tb, WIDTH)
    o_ref[...] = (h * jax.nn.gelu(ga_ref[...])).astype(_BF)


def _rglru(proj, cw, cb, wa, ba, wx, bx, lam, layer, B, S, tb):
    nb = S // tb
    hb = tb // CONV_HALO
    vec = pl.BlockSpec((None, 1, WIDTH), lambda b, i: (layer, 0, 0))
    mat = pl.BlockSpec((None, WIDTH // RNN_BLOCK, RNN_BLOCK, RNN_BLOCK), lambda b, i: (layer, 0, 0, 0))
    return pl.pallas_call(
        functools.partial(_rglru_kernel, tb=tb),
        grid=(B, nb),
        in_specs=[
            pl.BlockSpec((tb, WIDTH), lambda b, i: (b * nb + i, NAT_XA // WIDTH)),
            pl.BlockSpec((CONV_HALO, WIDTH),
                         lambda b, i: (jnp.maximum((b * nb + i) * hb - 1, 0), NAT_XA // WIDTH)),
            pl.BlockSpec((tb, WIDTH), lambda b, i: (b * nb + i, NAT_GA // WIDTH)),
            pl.BlockSpec((None, CONV_WIDTH, WIDTH), lambda b, i: (layer, 0, 0)),
            vec, mat, vec, mat, vec, vec,
        ],
        out_specs=pl.BlockSpec((tb, WIDTH), lambda b, i: (b * nb + i, 0)),
        out_shape=jax.ShapeDtypeStruct((B * S, WIDTH), _BF),
        scratch_shapes=[pltpu.VMEM((8, WIDTH), _F32),
                        pltpu.VMEM((tb // SCAN_GROUP, SCAN_GROUP, WIDTH), _F32),
                        pltpu.VMEM((tb // SCAN_GROUP, SCAN_GROUP, WIDTH), _F32),
                        pltpu.VMEM((tb + CONV_HALO, WIDTH), _F32)],
        compiler_params=_params("parallel", "arbitrary"),
        name="rglru",
    )(proj, proj, proj, cw, cb, wa, ba, wx, bx, lam)


def _pool_kernel(xb_ref, halo_ref, pw_ref, sc_ref, o_ref, *, tb):
    i = pl.program_id(1)
    x = xb_ref[...]
    halo = jnp.where(i == 0, 0.0, halo_ref[...])
    xe = jnp.concatenate([halo, x], axis=0)
    pos = i * tb + lax.broadcasted_iota(jnp.int32, (tb, 1), 0)
    for g, win in enumerate(POOL_WINDOWS):
        sl = slice(g * POOL_GROUP, (g + 1) * POOL_GROUP)
        s = xe[:, sl]
        sh = 1
        while sh < win:
            s = s + pltpu.roll(s, sh, axis=0)
            sh *= 2
        cnt = jnp.minimum(pos + 1, win).astype(_F32)
        pooled = s[POOL_HALO:] / cnt - x[:, sl]
        y = _dot(pooled.astype(_BF), pw_ref[g])
        o_ref[:, sl] = (y * sc_ref[:, sl]).astype(_BF)


def _pool(proj, pw, sc, layer, B, S, tb):
    nb = S // tb
    hb = tb // POOL_HALO
    return pl.pallas_call(
        functools.partial(_pool_kernel, tb=tb),
        grid=(B, nb),
        in_specs=[
            pl.BlockSpec((tb, WIDTH), lambda b, i: (b * nb + i, NAT_XB // WIDTH)),
            pl.BlockSpec((POOL_HALO, WIDTH),
                         lambda b, i: (jnp.maximum((b * nb + i) * hb - 1, 0), NAT_XB // WIDTH)),
            pl.BlockSpec((None, len(POOL_WINDOWS), POOL_GROUP, POOL_GROUP), lambda b, i: (layer, 0, 0, 0)),
            pl.BlockSpec((None, 1, WIDTH), lambda b, i: (layer, 0, 0)),
        ],
        out_specs=pl.BlockSpec((tb, WIDTH), lambda b, i: (b * nb + i, 0)),
        out_shape=jax.ShapeDtypeStruct((B * S, WIDTH), _BF),
        compiler_params=_params("parallel", "parallel"),
        name="pool",
    )(proj, proj, pw, sc)


def _attn_order(g):
    slab_res = np.empty(ATTN_SLABS, np.int64)
    for r in range(ATTN_SLABS):
        slab_res[_slab_of_residue(r)] = r
    if g == 0:
        qi = (slab_res[:, None] + ATTN_SLABS * np.arange(8)[None, :]).reshape(-1)
        ki = (slab_res[:, None] + ATTN_SLABS * np.arange(16)[None, :]).reshape(-1)
    elif g == 1:
        qi = (np.arange(4)[:, None] + 4 * np.arange(32)[None, :]).reshape(-1)
        ki = (np.arange(4)[:, None] + 4 * np.arange(64)[None, :]).reshape(-1)
    else:
        qi = np.arange(ATTN_BLOCK)
        ki = np.arange(2 * ATTN_BLOCK)
    return qi, ki


def _attn_bias():
    out = np.empty((2 * len(ATTN_DILATIONS), ATTN_BLOCK, 2 * ATTN_BLOCK), np.float32)
    for g in range(len(ATTN_DILATIONS)):
        qi, ki = _attn_order(g)
        band = (ki[None, :] >= qi[:, None]) & (ki[None, :] <= qi[:, None] + ATTN_BLOCK)
        out[2 * g] = np.where(band, 0.0, NEG_INF)
        out[2 * g + 1] = np.where(band & (ki[None, :] >= ATTN_BLOCK), 0.0, NEG_INF)
    return out


def _attn_kernel(q0_ref, q1_ref, q2_ref, kc_ref, kp_ref, vc_ref, vp_ref, bias_ref, o_ref,
                 ks_ref, vs_ref, o0_ref, o1_ref, o2_ref, l0_ref, l1_ref, l2_ref):
    n = pl.program_id(2)
    NJ = ATTN_BLOCK
    ks_ref[:, NJ:, :] = kc_ref[...]
    vs_ref[:, NJ:, :] = vc_ref[...]

    @pl.when(n == 0)
    def _():
        ks_ref[:, :NJ, :] = jnp.zeros((ATTN_SLABS, NJ, HEAD_DIM), _F32)
        vs_ref[:, :NJ, :] = jnp.zeros((ATTN_SLABS, NJ, HEAD_DIM), _F32)

    @pl.when(n > 0)
    def _():
        ks_ref[:, :NJ, :] = kp_ref[...]
        vs_ref[:, :NJ, :] = vp_ref[...]

    scale = HEAD_DIM ** -0.5
    exp2_scale = scale * 1.4426950408889634

    shapes = ((ATTN_SLABS, 8), (4, 32), (1, ATTN_BLOCK))
    for g, (q_ref, og_ref, lg_ref) in enumerate(((q0_ref, o0_ref, l0_ref), (q1_ref, o1_ref, l1_ref),
                                                 (q2_ref, o2_ref, l2_ref))):
        ns, nj = shapes[g]
        classes = ATTN_SLABS // ns

        def body(idx, carry, g=g, q_ref=q_ref, og_ref=og_ref, lg_ref=lg_ref, ns=ns, nj=nj, classes=classes):
            c = idx % classes
            m = idx // classes
            slabs = pl.ds(c * ns, ns)
            j0 = pl.multiple_of(m * nj, 8)
            q = q_ref[slabs, pl.ds(j0, nj), :].reshape(ATTN_BLOCK, HEAD_DIM).astype(_BF)
            kj = pl.ds(pl.multiple_of(NJ - nj + m * nj, 8), 2 * nj)
            k = ks_ref[slabs, kj, :].reshape(2 * ATTN_BLOCK, HEAD_DIM).astype(_BF)
            v = vs_ref[slabs, kj, :].reshape(2 * ATTN_BLOCK, HEAD_DIM).astype(_BF)
            first = jnp.logical_and(n == 0, m == 0).astype(jnp.int32)
            s = (lax.dot_general(q, k, (((1,), (1,)), ((), ())), preferred_element_type=_F32)
                 + bias_ref[2 * g + first])
            mx = jnp.max(s, axis=-1, keepdims=True)
            p = jnp.exp2((s - mx) * exp2_scale)
            l = jnp.sum(p, axis=-1, keepdims=True)
            o = _dot(p.astype(_BF), v) / l
            lse = mx * scale + jnp.log(l)
            og_ref[slabs, pl.ds(j0, nj), :] = o.reshape(ns, nj, HEAD_DIM)
            lg_ref[slabs, pl.ds(j0, nj), :] = jnp.broadcast_to(lse, (ATTN_BLOCK, HEAD_DIM)).reshape(ns, nj, HEAD_DIM)
            return carry

        lax.fori_loop(0, ATTN_TILE // ATTN_BLOCK, body, 0, unroll=ATTN_UNROLL)

    l0, l1, l2 = l0_ref[...], l1_ref[...], l2_ref[...]
    mx = jnp.maximum(jnp.maximum(l0, l1), l2)
    w0, w1, w2 = jnp.exp(l0 - mx), jnp.exp(l1 - mx), jnp.exp(l2 - mx)
    out = (w0 * o0_ref[...] + w1 * o1_ref[...] + w2 * o2_ref[...]) / (w0 + w1 + w2)
    for r in range(ATTN_SLABS):
        o_ref[pl.ds(r, NJ, stride=ATTN_SLABS), :] = out[_slab_of_residue(r)]


def _attn(qkv, B, S):
    nt = S // ATTN_TILE
    qb, kb, vb = QKV_Q // HEAD_DIM, QKV_K // HEAD_DIM, QKV_V // HEAD_DIM
    blk = (None, ATTN_SLABS, ATTN_BLOCK, HEAD_DIM)

    def cur(col0):
        return pl.BlockSpec(blk, lambda b, h, n: (b * nt + n, 0, 0, col0 + h))

    def prev(col0):
        return pl.BlockSpec(blk, lambda b, h, n: (b * nt + jnp.maximum(n - 1, 0), 0, 0, col0 + h))

    n_bias = 2 * len(ATTN_DILATIONS)
    big = pltpu.VMEM((ATTN_SLABS, 2 * ATTN_BLOCK, HEAD_DIM), _F32)
    tile = pltpu.VMEM((ATTN_SLABS, ATTN_BLOCK, HEAD_DIM), _F32)
    return pl.pallas_call(
        _attn_kernel,
        grid=(B, KV_HEADS, nt),
        in_specs=[cur(qb), cur(qb + KV_HEADS), cur(qb + 2 * KV_HEADS),
                  cur(kb), prev(kb), cur(vb), prev(vb),
                  pl.BlockSpec((n_bias, ATTN_BLOCK, 2 * ATTN_BLOCK), lambda b, h, n: (0, 0, 0))],
        out_specs=pl.BlockSpec((ATTN_TILE, HEAD_DIM), lambda b, h, n: (b * nt + n, h)),
        out_shape=jax.ShapeDtypeStruct((B * S, WIDTH), _F32),
        scratch_shapes=[big, big, tile, tile, tile, tile, tile, tile],
        compiler_params=_params("parallel", "parallel", "parallel"),
        name="attn",
    )(qkv, qkv, qkv, qkv, qkv, qkv, qkv, jnp.asarray(_attn_bias()))


def _sgu_kernel(z_ref, g_ref, ws_ref, b_ref, o_ref, *, tb):
    gz = jax.nn.gelu(z_ref[...])
    u = gz[:, :WIDTH]
    vv = _rms(gz[:, WIDTH:], g_ref[...]).astype(_BF)
    row = lax.broadcasted_iota(jnp.int32, (SG_CHUNK, SG_CHUNK), 0)
    col = lax.broadcasted_iota(jnp.int32, (SG_CHUNK, SG_CHUNK), 1)
    tri = row >= col
    for g in range(SG_GROUPS):
        w = jnp.where(tri, ws_ref[g], 0.0).astype(_BF)
        cs = slice(g * SG_CHUNK, (g + 1) * SG_CHUNK)
        for c in range(tb // SG_CHUNK):
            rs = slice(c * SG_CHUNK, (c + 1) * SG_CHUNK)
            mixed = _dot(w, vv[rs, cs]) + b_ref[g]
            o_ref[rs, cs] = (u[rs, cs] * mixed).astype(_BF)


def _sgu(proj, sg_norm, sg_w, sg_b, layer, T, tb):
    return pl.pallas_call(
        functools.partial(_sgu_kernel, tb=tb),
        grid=(T // tb,),
        in_specs=[
            pl.BlockSpec((tb, 2 * WIDTH), lambda i: (i, NAT_ZD // (2 * WIDTH))),
            pl.BlockSpec((None, 1, WIDTH), lambda i: (layer, 0, 0)),
            pl.BlockSpec((None, SG_GROUPS, SG_CHUNK, SG_CHUNK), lambda i: (layer, 0, 0, 0)),
            pl.BlockSpec((None, SG_GROUPS, SG_CHUNK, 1), lambda i: (layer, 0, 0, 0)),
        ],
        out_specs=pl.BlockSpec((tb, WIDTH), lambda i: (i, 0)),
        out_shape=jax.ShapeDtypeStruct((T, WIDTH), _BF),
        compiler_params=_params("parallel"),
        name="sgu",
    )(proj, sg_norm, sg_w, sg_b)


def _merge_kernel(x_ref, g_ref, ya_ref, yb_ref, yc_ref, yd_ref, wg0_ref, wg1_ref, wg2_ref, wg3_ref,
                  bg_ref, wb_ref, wo_ref, o_ref, h_ref):
    @pl.when(pl.program_id(1) == 0)
    def _():
        x = x_ref[...]
        h_ref[...] = _rms(x, g_ref[...]).astype(_BF)
        o_ref[...] = x

    h = h_ref[...]
    merged = None
    for b, (y_ref, wg_ref) in enumerate(((ya_ref, wg0_ref), (yb_ref, wg1_ref),
                                         (yc_ref, wg2_ref), (yd_ref, wg3_ref))):
        gate = jax.nn.sigmoid(_dot(h, wg_ref[...]) + bg_ref[b:b + 1, :])
        term = gate * _dot(y_ref[...].astype(_BF), wb_ref[b])
        merged = term if merged is None else merged + term
    o_ref[...] += _dot(merged.astype(_BF), wo_ref[...])


def _merge(x2, norm, ys, w_in, b_gate, w_branch, w_out, layer, tm, tn):
    T, D = x2.shape
    y_spec = pl.BlockSpec((tm, WIDTH), lambda i, j: (i, 0))

    def gate_spec(b):
        col0 = (COL_GATES + b * D) // tn
        return pl.BlockSpec((None, D, tn), lambda i, j: (layer, 0, col0 + j))

    return pl.pallas_call(
        _merge_kernel,
        grid=(T // tm, D // tn),
        in_specs=[
            pl.BlockSpec((tm, D), lambda i, j: (i, 0)),
            pl.BlockSpec((None, 1, D), lambda i, j: (layer, 0, 0)),
            y_spec, y_spec, y_spec, y_spec,
            gate_spec(0), gate_spec(1), gate_spec(2), gate_spec(3),
            pl.BlockSpec((None, N_BRANCH, tn), lambda i, j: (layer, 0, j)),
            pl.BlockSpec((None, N_BRANCH, WIDTH, tn), lambda i, j: (layer, 0, 0, j)),
            pl.BlockSpec((None, tn, D), lambda i, j: (layer, j, 0)),
        ],
        out_specs=pl.BlockSpec((tm, D), lambda i, j: (i, 0)),
        out_shape=jax.ShapeDtypeStruct((T, D), _F32),
        scratch_shapes=[pltpu.VMEM((tm, D), _BF)],
        compiler_params=_params("parallel", "arbitrary"),
        name="merge",
    )(x2, norm, *ys, w_in, w_in, w_in, w_in, b_gate, w_branch, w_out)


def _block_diag(w):
    L = w.shape[0]
    per = RNN_BLOCK // RNN_HEAD_DIM
    w = w.reshape(L, RNN_HEADS // per, per, RNN_HEAD_DIM, RNN_HEAD_DIM)
    eye = jnp.eye(per, dtype=w.dtype)
    bd = jnp.einsum('lphij,hk->lphikj', w, eye)
    return bd.reshape(L, RNN_HEADS // per, RNN_BLOCK, RNN_BLOCK).astype(_BF)


def kernel(x, p, ffn1_norm, ffn1_w1, ffn1_w3, ffn1_w2, mix_norm, w_in, b_gate, conv_w, conv_b, rg_wa, rg_ba, rg_wx, rg_bx, rg_lambda, pool_w, pool_scale, q_gain, k_gain, sg_norm, sg_w, sg_b, w_branch, w_out, ffn2_norm, ffn2_w1, ffn2_w3, ffn2_w2, ple_norm, ple_gate_w, ple_proj):
    B, S, D = x.shape
    L = w_in.shape[0]
    T = B * S
    assert D == D_MODEL and S % ATTN_TILE == 0
    tm = 512

    def vec(a):
        return a.reshape(L, 1, a.shape[-1])

    f1 = (ffn1_w1.astype(_BF), ffn1_w3.astype(_BF), ffn1_w2.astype(_BF))
    f2 = (ffn2_w1.astype(_BF), ffn2_w3.astype(_BF), ffn2_w2.astype(_BF))
    w_in_b = w_in.astype(_BF)
    w_branch_b = w_branch.astype(_BF)
    w_out_b = w_out.astype(_BF)
    ple_gate_b = ple_gate_w.astype(_BF)
    ple_proj_b = ple_proj.astype(_BF)
    pool_w_b = pool_w.astype(_BF)
    wa_bd, wx_bd = _block_diag(rg_wa), _block_diag(rg_wx)
    ones = jnp.ones((L, COL_Q), _F32)
    qk_gain = jnp.concatenate(
        [ones, jnp.tile(q_gain, (1, (COL_K - COL_Q) // HEAD_DIM)),
         jnp.tile(k_gain, (1, (COL_V - COL_K) // HEAD_DIM)),
         jnp.ones((L, MIX_COLS - COL_V), _F32)], axis=1).reshape(L, 1, MIX_COLS)
    p2 = p.reshape(L, T, PLE_DIM)
    sg_b4 = sg_b.reshape(L, SG_GROUPS, SG_CHUNK, 1)

    x2 = x.reshape(T, D)
    for i in range(L):
        x2 = _ffn(x2, vec(ffn1_norm), *f1, i, 1024)
        proj, qkv = _proj(x2, vec(mix_norm), w_in_b, qk_gain, i, 1024, 512)
        ya = _rglru(proj, conv_w, vec(conv_b), wa_bd, vec(rg_ba), wx_bd, vec(rg_bx), vec(rg_lambda),
                    i, B, S, 256)
        yb = _pool(proj, pool_w_b, vec(pool_scale), i, B, S, 512)
        yc = _attn(qkv, B, S)
        yd = _sgu(proj, vec(sg_norm), sg_w, sg_b4, i, T, 512)
        x2 = _merge(x2, vec(mix_norm), (ya, yb, yc, yd), w_in_b, b_gate, w_branch_b, w_out_b, i, tm, 256)
        x2 = _ffn(x2, vec(ffn2_norm), *f2, i, 512, ple=(vec(ple_norm), p2, ple_gate_b, ple_proj_b))
    return x2.reshape(B, S, D)
```

```python
import functools

import numpy as np

import jax
import jax.numpy as jnp
from jax import lax
from jax.experimental import pallas as pl
from jax.experimental.pallas import tpu as pltpu

EPS = 1e-6
NEG_INF = -1e30
D_MODEL = 2048
D_FF = 5504
FF_TILE = 512
PLE_TILE = 512
PLE_DIM = 256
WIDTH = 1024
RNN_HEADS = 16
RNN_HEAD_DIM = 64
RNN_BLOCK = 256
RG_C = 8.0
CONV_WIDTH = 4
POOL_WINDOWS = (2, 4, 8, 16)
POOL_GROUP = 256
POOL_HALO = 16
CONV_HALO = 8
SCAN_GROUP = 8
HEAD_DIM = 128
KV_HEADS = 8
ATTN_DILATIONS = (1, 4, 16)
ATTN_BLOCK = 128
ATTN_TILE = ATTN_BLOCK * max(ATTN_DILATIONS)
ATTN_UNROLL = 16
SG_CHUNK = 128
SG_GROUPS = 8
N_BRANCH = 4
COL_XA, COL_GA, COL_XB, COL_Q, COL_K, COL_V, COL_ZD, COL_GATES = (
    0, 1024, 2048, 3072, 6144, 7168, 8192, 10240)
MIX_COLS = COL_GATES
NAT_ZD, NAT_XA, NAT_GA, NAT_XB, NAT_COLS = 0, 2048, 3072, 4096, 5120
QKV_Q, QKV_K, QKV_V, QKV_COLS = 0, 3072, 4096, 5120
ATTN_SLABS = 16
PERM_ROWS = 256
VMEM_LIMIT = 56 * 1024 * 1024

_BF = jnp.bfloat16
_F32 = jnp.float32


def _params(*sem):
    return pltpu.CompilerParams(dimension_semantics=sem, vmem_limit_bytes=VMEM_LIMIT)


def _rms(x, g):
    return x * lax.rsqrt(jnp.mean(x * x, axis=-1, keepdims=True) + EPS) * g


def _dot(a, b):
    return jnp.dot(a, b, preferred_element_type=_F32)


def _ffn_kernel(x_ref, g_ref, w1_ref, w3_ref, w2_ref, o_ref, h_ref, *, nf):
    j = pl.program_id(1)
    tf = w1_ref.shape[1]

    @pl.when(j == 0)
    def _():
        x = x_ref[...]
        h_ref[...] = _rms(x, g_ref[...]).astype(_BF)
        o_ref[...] = x

    def ffn_step(valid):
        h = h_ref[...]
        a = _dot(h, w1_ref[:, :valid])
        b = _dot(h, w3_ref[:, :valid])
        act = (0.5 * (a * jax.nn.sigmoid(a)) * b).astype(_BF)
        for c in range(o_ref.shape[1] // FF_TILE):
            sl = slice(c * FF_TILE, (c + 1) * FF_TILE)
            o_ref[:, sl] += _dot(act, w2_ref[:valid, sl])

    last_valid = D_FF - (nf - 1) * tf
    if last_valid == tf:
        ffn_step(tf)
    else:
        pl.when(j < nf - 1)(lambda: ffn_step(tf))
        pl.when(j == nf - 1)(lambda: ffn_step(last_valid))


def _ffn(x2, norm, w1, w3, w2, layer, tm):
    T, D = x2.shape
    tf = FF_TILE
    nf = -(-D_FF // tf)
    return pl.pallas_call(
        functools.partial(_ffn_kernel, nf=nf),
        grid=(T // tm, nf),
        in_specs=[
            pl.BlockSpec((tm, D), lambda i, j: (i, 0)),
            pl.BlockSpec((None, 1, D), lambda i, j: (layer, 0, 0)),
            pl.BlockSpec((None, D, tf), lambda i, j: (layer, 0, j)),
            pl.BlockSpec((None, D, tf), lambda i, j: (layer, 0, j)),
            pl.BlockSpec((None, tf, D), lambda i, j: (layer, j, 0)),
        ],
        out_specs=pl.BlockSpec((tm, D), lambda i, j: (i, 0)),
        out_shape=jax.ShapeDtypeStruct((T, D), _F32),
        scratch_shapes=[pltpu.VMEM((tm, D), _BF)],
        compiler_params=_params("parallel", "arbitrary"),
        name="ffn",
    )(x2, norm, w1, w3, w2)


def _ple_kernel(x_ref, g_ref, p_ref, wg_ref, wp_ref, o_ref, h_ref):
    j = pl.program_id(1)

    @pl.when(j == 0)
    def _():
        h_ref[...] = _rms(x_ref[...], g_ref[...]).astype(_BF)

    gate = jax.nn.sigmoid(_dot(h_ref[...], wg_ref[...]))
    upd = gate * _dot(p_ref[...].astype(_BF), wp_ref[...])
    tn = o_ref.shape[1]
    for c in range(x_ref.shape[1] // tn):
        @pl.when(j == c)
        def _(c=c):
            o_ref[...] = x_ref[:, c * tn:(c + 1) * tn] + upd


def _ple(x2, norm, p, w_gate, w_proj, layer, tm, tn):
    T, D = x2.shape
    return pl.pallas_call(
        _ple_kernel,
        grid=(T // tm, D // tn),
        in_specs=[
            pl.BlockSpec((tm, D), lambda i, j: (i, 0)),
            pl.BlockSpec((None, 1, D), lambda i, j: (layer, 0, 0)),
            pl.BlockSpec((None, tm, PLE_DIM), lambda i, j: (layer, i, 0)),
            pl.BlockSpec((None, D, tn), lambda i, j: (layer, 0, j)),
            pl.BlockSpec((None, PLE_DIM, tn), lambda i, j: (layer, 0, j)),
        ],
        out_specs=pl.BlockSpec((tm, tn), lambda i, j: (i, j)),
        out_shape=jax.ShapeDtypeStruct((T, D), _F32),
        scratch_shapes=[pltpu.VMEM((tm, D), _BF)],
        compiler_params=_params("parallel", "arbitrary"),
        name="ple",
    )(x2, norm, p, w_gate, w_proj)


def _slab_of_residue(r):
    return (r % 4) * 4 + r // 4


def _proj_kernel(x_ref, g_ref, w_ref, gain_ref, nat_ref, qkv_ref, h_ref, hp_ref, *, n_nat, n_norm):
    j = pl.program_id(1)
    tm = x_ref.shape[0]
    nsub = tm // PERM_ROWS

    @pl.when(j == 0)
    def _():
        h = _rms(x_ref[...], g_ref[...]).astype(_BF)
        h_ref[...] = h
        lam = lax.broadcasted_iota(jnp.int32, (PERM_ROWS, PERM_ROWS), 0)
        tau = lax.broadcasted_iota(jnp.int32, (PERM_ROWS, PERM_ROWS), 1)
        r = tau % ATTN_SLABS
        perm = (lam == _slab_of_residue(r) * (PERM_ROWS // ATTN_SLABS) + tau // ATTN_SLABS).astype(_BF)
        for s in range(nsub):
            rows = slice(s * PERM_ROWS, (s + 1) * PERM_ROWS)
            hp_ref[rows, :] = _dot(perm, h[rows, :]).astype(_BF)

    @pl.when(j < n_nat)
    def _():
        nat_ref[...] = _dot(h_ref[...], w_ref[...])

    def store_qkv(cols, y):
        run = PERM_ROWS // ATTN_SLABS
        for s in range(nsub):
            for slab in range(ATTN_SLABS):
                r0 = s * PERM_ROWS + slab * run
                qkv_ref[slab, s * run:(s + 1) * run, cols] = y[r0:r0 + run, :]

    is_qk = jnp.logical_and(j >= n_nat, j < n_nat + n_norm)

    @pl.when(is_qk)
    def _():
        acc = _dot(hp_ref[...], w_ref[...])
        for c in range(acc.shape[1] // HEAD_DIM):
            sl = slice(c * HEAD_DIM, (c + 1) * HEAD_DIM)
            store_qkv(sl, _rms(acc[:, sl], gain_ref[:, sl]))

    @pl.when(j >= n_nat + n_norm)
    def _():
        store_qkv(slice(None), _dot(hp_ref[...], w_ref[...]))


def _proj(x2, norm, w_in, gain, layer, tm, tn):
    T, D = x2.shape
    n_zd = (COL_GATES - COL_ZD) // tn
    n_nat = n_zd + COL_Q // tn
    n_norm = (COL_V - COL_Q) // tn
    n_all = MIX_COLS // tn
    assert tm % PERM_ROWS == 0 and ATTN_TILE % tm == 0
    per_tile = ATTN_TILE // tm

    def w_col(j):
        return jnp.where(j < n_zd, j + COL_ZD // tn, j - n_zd)

    return pl.pallas_call(
        functools.partial(_proj_kernel, n_nat=n_nat, n_norm=n_norm),
        grid=(T // tm, n_all),
        in_specs=[
            pl.BlockSpec((tm, D), lambda i, j: (i, 0)),
            pl.BlockSpec((None, 1, D), lambda i, j: (layer, 0, 0)),
            pl.BlockSpec((None, D, tn), lambda i, j: (layer, 0, w_col(j))),
            pl.BlockSpec((None, 1, tn), lambda i, j: (layer, 0, w_col(j))),
        ],
        out_specs=[
            pl.BlockSpec((tm, tn), lambda i, j: (i, jnp.minimum(j, n_nat - 1))),
            pl.BlockSpec((None, ATTN_SLABS, tm // ATTN_SLABS, tn),
                         lambda i, j: (i // per_tile, 0, i % per_tile, jnp.maximum(j - n_nat, 0))),
        ],
        out_shape=[jax.ShapeDtypeStruct((T, NAT_COLS), _F32),
                   jax.ShapeDtypeStruct((T // ATTN_TILE, ATTN_SLABS, ATTN_BLOCK, QKV_COLS), _F32)],
        scratch_shapes=[pltpu.VMEM((tm, D), _BF), pltpu.VMEM((tm, D), _BF)],
        compiler_params=_params("parallel", "arbitrary"),
        name="proj",
    )(x2, norm, w_in, gain)


def _rglru_kernel(xa_ref, halo_ref, ga_ref, cw_ref, cb_ref, wa_ref, ba_ref, wx_ref, bx_ref,
                  lam_ref, o_ref, carry_ref, sa_ref, sb_ref, xe_ref, *, tb):
    i = pl.program_id(1)

    @pl.when(i == 0)
    def _():
        carry_ref[...] = jnp.zeros_like(carry_ref)

    xe_ref[:CONV_HALO, :] = jnp.where(i == 0, 0.0, halo_ref[...])
    xe_ref[CONV_HALO:, :] = xa_ref[...]
    cw = cw_ref[...]
    xc = cb_ref[...] + cw[0:1] * xa_ref[...]
    for j in range(1, CONV_WIDTH):
        xc = xc + cw[j:j + 1] * xe_ref[pl.ds(CONV_HALO - j, tb), :]
    xcb = xc.astype(_BF)
    r_lin, i_lin = [], []
    for p in range(WIDTH // RNN_BLOCK):
        sl = slice(p * RNN_BLOCK, (p + 1) * RNN_BLOCK)
        r_lin.append(_dot(xcb[:, sl], wa_ref[p]))
        i_lin.append(_dot(xcb[:, sl], wx_ref[p]))
    r = jax.nn.sigmoid(jnp.concatenate(r_lin, axis=1) + ba_ref[...])
    ig = jax.nn.sigmoid(jnp.concatenate(i_lin, axis=1) + bx_ref[...])
    z = -lam_ref[...]
    softplus = jnp.maximum(z, 0.0) + jnp.log1p(jnp.exp(-jnp.abs(z)))
    log_a = (-RG_C * r) * softplus
    a = jnp.exp(log_a)
    y = 1.0 - a * a
    b = jnp.where(y > 0.0, y * lax.rsqrt(y), 0.0) * (ig * xc)
    ng = tb // SCAN_GROUP
    a = a.reshape(ng, SCAN_GROUP, WIDTH)
    b = b.reshape(ng, SCAN_GROUP, WIDTH)
    row = lax.broadcasted_iota(jnp.int32, (1, SCAN_GROUP, 1), 1)
    s = 1
    while s < SCAN_GROUP:
        keep = row >= s
        a_prev = jnp.where(keep, pltpu.roll(a, s, axis=1), 1.0)
        b_prev = jnp.where(keep, pltpu.roll(b, s, axis=1), 0.0)
        b = a * b_prev + b
        a = a * a_prev
        s *= 2
    sa_ref[...] = a
    sb_ref[...] = b

    def chain(g, carry):
        hg = sa_ref[g] * carry + sb_ref[g]
        sb_ref[g] = hg
        return hg[SCAN_GROUP - 1:SCAN_GROUP, :]

    carry_ref[0:1, :] = lax.fori_loop(0, ng, chain, carry_ref[0:1, :], unroll=8)
    h = sb_ref[...].reshape(tb, WIDTH)
    o_ref[...] = (h * jax.nn.gelu(ga_ref[...])).astype(_BF)


def _rglru(proj, cw, cb, wa, ba, wx, bx, lam, layer, B, S, tb):
    nb = S // tb
    hb = tb // CONV_HALO
    vec = pl.BlockSpec((None, 1, WIDTH), lambda b, i: (layer, 0, 0))
    mat = pl.BlockSpec((None, WIDTH // RNN_BLOCK, RNN_BLOCK, RNN_BLOCK), lambda b, i: (layer, 0, 0, 0))
    return pl.pallas_call(
        functools.partial(_rglru_kernel, tb=tb),
        grid=(B, nb),
        in_specs=[
            pl.BlockSpec((tb, WIDTH), lambda b, i: (b * nb + i, NAT_XA // WIDTH)),
            pl.BlockSpec((CONV_HALO, WIDTH),
                         lambda b, i: (jnp.maximum((b * nb + i) * hb - 1, 0), NAT_XA // WIDTH)),
            pl.BlockSpec((tb, WIDTH), lambda b, i: (b * nb + i, NAT_GA // WIDTH)),
            pl.BlockSpec((None, CONV_WIDTH, WIDTH), lambda b, i: (layer, 0, 0)),
            vec, mat, vec, mat, vec, vec,
        ],
        out_specs=pl.BlockSpec((tb, WIDTH), lambda b, i: (b * nb + i, 0)),
        out_shape=jax.ShapeDtypeStruct((B * S, WIDTH), _BF),
        scratch_shapes=[pltpu.VMEM((8, WIDTH), _F32),
                        pltpu.VMEM((tb // SCAN_GROUP, SCAN_GROUP, WIDTH), _F32),
                        pltpu.VMEM((tb // SCAN_GROUP, SCAN_GROUP, WIDTH), _F32),
                        pltpu.VMEM((tb + CONV_HALO, WIDTH), _F32)],
        compiler_params=_params("parallel", "arbitrary"),
        name="rglru",
    )(proj, proj, proj, cw, cb, wa, ba, wx, bx, lam)


def _pool_kernel(xb_ref, halo_ref, pw_ref, sc_ref, o_ref, *, tb):
    i = pl.program_id(1)
    x = xb_ref[...]
    halo = jnp.where(i == 0, 0.0, halo_ref[...])
    xe = jnp.concatenate([halo, x], axis=0)
    pos = i * tb + lax.broadcasted_iota(jnp.int32, (tb, 1), 0)
    for g, win in enumerate(POOL_WINDOWS):
        sl = slice(g * POOL_GROUP, (g + 1) * POOL_GROUP)
        s = xe[:, sl]
        sh = 1
        while sh < win:
            s = s + pltpu.roll(s, sh, axis=0)
            sh *= 2
        cnt = jnp.minimum(pos + 1, win).astype(_F32)
        pooled = s[POOL_HALO:] / cnt - x[:, sl]
        y = _dot(pooled.astype(_BF), pw_ref[g])
        o_ref[:, sl] = (y * sc_ref[:, sl]).astype(_BF)


def _pool(proj, pw, sc, layer, B, S, tb):
    nb = S // tb
    hb = tb // POOL_HALO
    return pl.pallas_call(
        functools.partial(_pool_kernel, tb=tb),
        grid=(B, nb),
        in_specs=[
            pl.BlockSpec((tb, WIDTH), lambda b, i: (b * nb + i, NAT_XB // WIDTH)),
            pl.BlockSpec((POOL_HALO, WIDTH),
                         lambda b, i: (jnp.maximum((b * nb + i) * hb - 1, 0), NAT_XB // WIDTH)),
            pl.BlockSpec((None, len(POOL_WINDOWS), POOL_GROUP, POOL_GROUP), lambda b, i: (layer, 0, 0, 0)),
            pl.BlockSpec((None, 1, WIDTH), lambda b, i: (layer, 0, 0)),
        ],
        out_specs=pl.BlockSpec((tb, WIDTH), lambda b, i: (b * nb + i, 0)),
        out_shape=jax.ShapeDtypeStruct((B * S, WIDTH), _BF),
        compiler_params=_params("parallel", "parallel"),
        name="pool",
    )(proj, proj, pw, sc)


def _attn_order(g):
    slab_res = np.empty(ATTN_SLABS, np.int64)
    for r in range(ATTN_SLABS):
        slab_res[_slab_of_residue(r)] = r
    if g == 0:
        qi = (slab_res[:, None] + ATTN_SLABS * np.arange(8)[None, :]).reshape(-1)
        ki = (slab_res[:, None] + ATTN_SLABS * np.arange(16)[None, :]).reshape(-1)
    elif g == 1:
        qi = (np.arange(4)[:, None] + 4 * np.arange(32)[None, :]).reshape(-1)
        ki = (np.arange(4)[:, None] + 4 * np.arange(64)[None, :]).reshape(-1)
    else:
        qi = np.arange(ATTN_BLOCK)
        ki = np.arange(2 * ATTN_BLOCK)
    return qi, ki


def _attn_bias():
    out = np.empty((2 * len(ATTN_DILATIONS), ATTN_BLOCK, 2 * ATTN_BLOCK), np.float32)
    for g in range(len(ATTN_DILATIONS)):
        qi, ki = _attn_order(g)
        band = (ki[None, :] >= qi[:, None]) & (ki[None, :] <= qi[:, None] + ATTN_BLOCK)
        out[2 * g] = np.where(band, 0.0, NEG_INF)
        out[2 * g + 1] = np.where(band & (ki[None, :] >= ATTN_BLOCK), 0.0, NEG_INF)
    return out


def _attn_kernel(q0_ref, q1_ref, q2_ref, kc_ref, kp_ref, vc_ref, vp_ref, bias_ref, o_ref,
                 ks_ref, vs_ref, o0_ref, o1_ref, o2_ref, l0_ref, l1_ref, l2_ref):
    n = pl.program_id(2)
    NJ = ATTN_BLOCK
    ks_ref[:, NJ:, :] = kc_ref[...]
    vs_ref[:, NJ:, :] = vc_ref[...]

    @pl.when(n == 0)
    def _():
        ks_ref[:, :NJ, :] = jnp.zeros((ATTN_SLABS, NJ, HEAD_DIM), _F32)
        vs_ref[:, :NJ, :] = jnp.zeros((ATTN_SLABS, NJ, HEAD_DIM), _F32)

    @pl.when(n > 0)
    def _():
        ks_ref[:, :NJ, :] = kp_ref[...]
        vs_ref[:, :NJ, :] = vp_ref[...]

    scale = HEAD_DIM ** -0.5
    exp2_scale = scale * 1.4426950408889634

    shapes = ((ATTN_SLABS, 8), (4, 32), (1, ATTN_BLOCK))
    for g, (q_ref, og_ref, lg_ref) in enumerate(((q0_ref, o0_ref, l0_ref), (q1_ref, o1_ref, l1_ref),
                                                 (q2_ref, o2_ref, l2_ref))):
        ns, nj = shapes[g]
        classes = ATTN_SLABS // ns

        def body(idx, carry, g=g, q_ref=q_ref, og_ref=og_ref, lg_ref=lg_ref, ns=ns, nj=nj, classes=classes):
            c = idx % classes
            m = idx // classes
            slabs = pl.ds(c * ns, ns)
            j0 = pl.multiple_of(m * nj, 8)
            q = q_ref[slabs, pl.ds(j0, nj), :].reshape(ATTN_BLOCK, HEAD_DIM).astype(_BF)
            kj = pl.ds(pl.multiple_of(NJ - nj + m * nj, 8), 2 * nj)
            k = ks_ref[slabs, kj, :].reshape(2 * ATTN_BLOCK, HEAD_DIM).astype(_BF)
            v = vs_ref[slabs, kj, :].reshape(2 * ATTN_BLOCK, HEAD_DIM).astype(_BF)
            first = jnp.logical_and(n == 0, m == 0).astype(jnp.int32)
            s = (lax.dot_general(q, k, (((1,), (1,)), ((), ())), preferred_element_type=_F32)
                 + bias_ref[2 * g + first])
            mx = jnp.max(s, axis=-1, keepdims=True)
            p = jnp.exp2((s - mx) * exp2_scale)
            l = jnp.sum(p, axis=-1, keepdims=True)
            o = _dot(p.astype(_BF), v) / l
            lse = mx * scale + jnp.log(l)
            og_ref[slabs, pl.ds(j0, nj), :] = o.reshape(ns, nj, HEAD_DIM)
            lg_ref[slabs, pl.ds(j0, nj), :] = jnp.broadcast_to(lse, (ATTN_BLOCK, HEAD_DIM)).reshape(ns, nj, HEAD_DIM)
            return carry

        lax.fori_loop(0, ATTN_TILE // ATTN_BLOCK, body, 0, unroll=ATTN_UNROLL)

    l0, l1, l2 = l0_ref[...], l1_ref[...], l2_ref[...]
    mx = jnp.maximum(jnp.maximum(l0, l1), l2)
    w0, w1, w2 = jnp.exp(l0 - mx), jnp.exp(l1 - mx), jnp.exp(l2 - mx)
    out = (w0 * o0_ref[...] + w1 * o1_ref[...] + w2 * o2_ref[...]) / (w0 + w1 + w2)
    for r in range(ATTN_SLABS):
        o_ref[pl.ds(r, NJ, stride=ATTN_SLABS), :] = out[_slab_of_residue(r)]


def _attn(qkv, B, S):
    nt = S // ATTN_TILE
    qb, kb, vb = QKV_Q // HEAD_DIM, QKV_K // HEAD_DIM, QKV_V // HEAD_DIM
    blk = (None, ATTN_SLABS, ATTN_BLOCK, HEAD_DIM)

    def cur(col0):
        return pl.BlockSpec(blk, lambda b, h, n: (b * nt + n, 0, 0, col0 + h))

    def prev(col0):
        return pl.BlockSpec(blk, lambda b, h, n: (b * nt + jnp.maximum(n - 1, 0), 0, 0, col0 + h))

    n_bias = 2 * len(ATTN_DILATIONS)
    big = pltpu.VMEM((ATTN_SLABS, 2 * ATTN_BLOCK, HEAD_DIM), _F32)
    tile = pltpu.VMEM((ATTN_SLABS, ATTN_BLOCK, HEAD_DIM), _F32)
    return pl.pallas_call(
        _attn_kernel,
        grid=(B, KV_HEADS, nt),
        in_specs=[cur(qb), cur(qb + KV_HEADS), cur(qb + 2 * KV_HEADS),
                  cur(kb), prev(kb), cur(vb), prev(vb),
                  pl.BlockSpec((n_bias, ATTN_BLOCK, 2 * ATTN_BLOCK), lambda b, h, n: (0, 0, 0))],
        out_specs=pl.BlockSpec((ATTN_TILE, HEAD_DIM), lambda b, h, n: (b * nt + n, h)),
        out_shape=jax.ShapeDtypeStruct((B * S, WIDTH), _F32),
        scratch_shapes=[big, big, tile, tile, tile, tile, tile, tile],
        compiler_params=_params("parallel", "parallel", "parallel"),
        name="attn",
    )(qkv, qkv, qkv, qkv, qkv, qkv, qkv, jnp.asarray(_attn_bias()))


def _sgu_kernel(z_ref, g_ref, ws_ref, b_ref, o_ref, *, tb):
    gz = jax.nn.gelu(z_ref[...])
    u = gz[:, :WIDTH]
    vv = _rms(gz[:, WIDTH:], g_ref[...]).astype(_BF)
    row = lax.broadcasted_iota(jnp.int32, (SG_CHUNK, SG_CHUNK), 0)
    col = lax.broadcasted_iota(jnp.int32, (SG_CHUNK, SG_CHUNK), 1)
    tri = row >= col
    for g in range(SG_GROUPS):
        w = jnp.where(tri, ws_ref[g], 0.0).astype(_BF)
        cs = slice(g * SG_CHUNK, (g + 1) * SG_CHUNK)
        for c in range(tb // SG_CHUNK):
            rs = slice(c * SG_CHUNK, (c + 1) * SG_CHUNK)
            mixed = _dot(w, vv[rs, cs]) + b_ref[g]
            o_ref[rs, cs] = (u[rs, cs] * mixed).astype(_BF)


def _sgu(proj, sg_norm, sg_w, sg_b, layer, T, tb):
    return pl.pallas_call(
        functools.partial(_sgu_kernel, tb=tb),
        grid=(T // tb,),
        in_specs=[
            pl.BlockSpec((tb, 2 * WIDTH), lambda i: (i, NAT_ZD // (2 * WIDTH))),
            pl.BlockSpec((None, 1, WIDTH), lambda i: (layer, 0, 0)),
            pl.BlockSpec((None, SG_GROUPS, SG_CHUNK, SG_CHUNK), lambda i: (layer, 0, 0, 0)),
            pl.BlockSpec((None, SG_GROUPS, SG_CHUNK, 1), lambda i: (layer, 0, 0, 0)),
        ],
        out_specs=pl.BlockSpec((tb, WIDTH), lambda i: (i, 0)),
        out_shape=jax.ShapeDtypeStruct((T, WIDTH), _BF),
        compiler_params=_params("parallel"),
        name="sgu",
    )(proj, sg_norm, sg_w, sg_b)


def _merge_kernel(x_ref, g_ref, ya_ref, yb_ref, yc_ref, yd_ref, wg0_ref, wg1_ref, wg2_ref, wg3_ref,
                  bg_ref, wb_ref, wo_ref, o_ref, h_ref):
    @pl.when(pl.program_id(1) == 0)
    def _():
        x = x_ref[...]
        h_ref[...] = _rms(x, g_ref[...]).astype(_BF)
        o_ref[...] = x

    h = h_ref[...]
    merged = None
    for b, (y_ref, wg_ref) in enumerate(((ya_ref, wg0_ref), (yb_ref, wg1_ref),
                                         (yc_ref, wg2_ref), (yd_ref, wg3_ref))):
        gate = jax.nn.sigmoid(_dot(h, wg_ref[...]) + bg_ref[b:b + 1, :])
        term = gate * _dot(y_ref[...].astype(_BF), wb_ref[b])
        merged = term if merged is None else merged + term
    o_ref[...] += _dot(merged.astype(_BF), wo_ref[...])


def _merge(x2, norm, ys, w_in, b_gate, w_branch, w_out, layer, tm, tn):
    T, D = x2.shape
    y_spec = pl.BlockSpec((tm, WIDTH), lambda i, j: (i, 0))

    def gate_spec(b):
        col0 = (COL_GATES + b * D) // tn
        return pl.BlockSpec((None, D, tn), lambda i, j: (layer, 0, col0 + j))

    return pl.pallas_call(
        _merge_kernel,
        grid=(T // tm, D // tn),
        in_specs=[
            pl.BlockSpec((tm, D), lambda i, j: (i, 0)),
            pl.BlockSpec((None, 1, D), lambda i, j: (layer, 0, 0)),
            y_spec, y_spec, y_spec, y_spec,
            gate_spec(0), gate_spec(1), gate_spec(2), gate_spec(3),
            pl.BlockSpec((None, N_BRANCH, tn), lambda i, j: (layer, 0, j)),
            pl.BlockSpec((None, N_BRANCH, WIDTH, tn), lambda i, j: (layer, 0, 0, j)),
            pl.BlockSpec((None, tn, D), lambda i, j: (layer, j, 0)),
        ],
        out_specs=pl.BlockSpec((tm, D), lambda i, j: (i, 0)),
        out_shape=jax.ShapeDtypeStruct((T, D), _F32),
        scratch_shapes=[pltpu.VMEM((tm, D), _BF)],
        compiler_params=_params("parallel", "arbitrary"),
        name="merge",
    )(x2, norm, *ys, w_in, w_in, w_in, w_in, b_gate, w_branch, w_out)


def _block_diag(w):
    L = w.shape[0]
    per = RNN_BLOCK // RNN_HEAD_DIM
    w = w.reshape(L, RNN_HEADS // per, per, RNN_HEAD_DIM, RNN_HEAD_DIM)
    eye = jnp.eye(per, dtype=w.dtype)
    bd = jnp.einsum('lphij,hk->lphikj', w, eye)
    return bd.reshape(L, RNN_HEADS // per, RNN_BLOCK, RNN_BLOCK).astype(_BF)


def kernel(x, p, ffn1_norm, ffn1_w1, ffn1_w3, ffn1_w2, mix_norm, w_in, b_gate, conv_w, conv_b, rg_wa, rg_ba, rg_wx, rg_bx, rg_lambda, pool_w, pool_scale, q_gain, k_gain, sg_norm, sg_w, sg_b, w_branch, w_out, ffn2_norm, ffn2_w1, ffn2_w3, ffn2_w2, ple_norm, ple_gate_w, ple_proj):
    B, S, D = x.shape
    L = w_in.shape[0]
    T = B * S
    assert D == D_MODEL and S % ATTN_TILE == 0
    tm = 512

    def vec(a):
        return a.reshape(L, 1, a.shape[-1])

    f1 = (ffn1_w1.astype(_BF), ffn1_w3.astype(_BF), ffn1_w2.astype(_BF))
    f2 = (ffn2_w1.astype(_BF), ffn2_w3.astype(_BF), ffn2_w2.astype(_BF))
    w_in_b = w_in.astype(_BF)
    w_branch_b = w_branch.astype(_BF)
    w_out_b = w_out.astype(_BF)
    ple_gate_b = ple_gate_w.astype(_BF)
    ple_proj_b = ple_proj.astype(_BF)
    pool_w_b = pool_w.astype(_BF)
    wa_bd, wx_bd = _block_diag(rg_wa), _block_diag(rg_wx)
    ones = jnp.ones((L, COL_Q), _F32)
    qk_gain = jnp.concatenate(
        [ones, jnp.tile(q_gain, (1, (COL_K - COL_Q) // HEAD_DIM)),
         jnp.tile(k_gain, (1, (COL_V - COL_K) // HEAD_DIM)),
         jnp.ones((L, MIX_COLS - COL_V), _F32)], axis=1).reshape(L, 1, MIX_COLS)
    p2 = p.reshape(L, T, PLE_DIM)
    sg_b4 = sg_b.reshape(L, SG_GROUPS, SG_CHUNK, 1)

    x2 = x.reshape(T, D)
    for i in range(L):
        x2 = _ffn(x2, vec(ffn1_norm), *f1, i, 1024)
        proj, qkv = _proj(x2, vec(mix_norm), w_in_b, qk_gain, i, 1024, 512)
        ya = _rglru(proj, conv_w, vec(conv_b), wa_bd, vec(rg_ba), wx_bd, vec(rg_bx), vec(rg_lambda),
                    i, B, S, 256)
        yb = _pool(proj, pool_w_b, vec(pool_scale), i, B, S, 512)
        yc = _attn(qkv, B, S)
        yd = _sgu(proj, vec(sg_norm), sg_w, sg_b4, i, T, 512)
        x2 = _merge(x2, vec(mix_norm), (ya, yb, yc, yd), w_in_b, b_gate, w_branch_b, w_out_b, i, tm, 256)
        x2 = _ffn(x2, vec(ffn2_norm), *f2, i, 1024)
        x2 = _ple(x2, vec(ple_norm), p2, ple_gate_b, ple_proj_b, i, 1024, PLE_TILE)
    return x2.reshape(B, S, D)
```

```python
import functools

import numpy as np

import jax
import jax.numpy as jnp
from jax import lax
from jax.experimental import pallas as pl
from jax.experimental.pallas import tpu as pltpu

EPS = 1e-6
NEG_INF = -1e30
D_MODEL = 2048
D_FF = 5504
FF_TILE = 512
PLE_TILE = 1024
PLE_DIM = 256
WIDTH = 1024
RNN_HEADS = 16
RNN_HEAD_DIM = 64
RNN_BLOCK = 256
RG_C = 8.0
CONV_WIDTH = 4
POOL_WINDOWS = (2, 4, 8, 16)
POOL_GROUP = 256
POOL_HALO = 16
CONV_HALO = 8
SCAN_GROUP = 8
HEAD_DIM = 128
KV_HEADS = 8
ATTN_DILATIONS = (1, 4, 16)
ATTN_BLOCK = 128
ATTN_TILE = ATTN_BLOCK * max(ATTN_DILATIONS)
ATTN_UNROLL = 16
SG_CHUNK = 128
SG_GROUPS = 8
N_BRANCH = 4
COL_XA, COL_GA, COL_XB, COL_Q, COL_K, COL_V, COL_ZD, COL_GATES = (
    0, 1024, 2048, 3072, 6144, 7168, 8192, 10240)
MIX_COLS = COL_GATES
NAT_ZD, NAT_XA, NAT_GA, NAT_XB, NAT_COLS = 0, 2048, 3072, 4096, 5120
QKV_Q, QKV_K, QKV_V, QKV_COLS = 0, 3072, 4096, 5120
ATTN_SLABS = 16
PERM_ROWS = 256
VMEM_LIMIT = 56 * 1024 * 1024

_BF = jnp.bfloat16
_F32 = jnp.float32


def _params(*sem):
    return pltpu.CompilerParams(dimension_semantics=sem, vmem_limit_bytes=VMEM_LIMIT)


def _rms(x, g):
    return x * lax.rsqrt(jnp.mean(x * x, axis=-1, keepdims=True) + EPS) * g


def _dot(a, b):
    return jnp.dot(a, b, preferred_element_type=_F32)


def _x_spec(tm, d, n_tiles):
    last = n_tiles - 1
    return pl.BlockSpec((tm, d), lambda i, j: (jnp.where(j == 0, i, jnp.minimum(i + 1, last)), 0))


def _ffn_kernel(x_ref, g_ref, w1_ref, w3_ref, w2_ref, o_ref, h_ref, *, nf):
    j = pl.program_id(1)
    tf = w1_ref.shape[1]

    @pl.when(j == 0)
    def _():
        x = x_ref[...]
        h_ref[...] = _rms(x, g_ref[...]).astype(_BF)
        o_ref[...] = x

    def ffn_step(valid):
        h = h_ref[...]
        a = _dot(h, w1_ref[:, :valid])
        b = _dot(h, w3_ref[:, :valid])
        act = (0.5 * (a * jax.nn.sigmoid(a)) * b).astype(_BF)
        for c in range(o_ref.shape[1] // FF_TILE):
            sl = slice(c * FF_TILE, (c + 1) * FF_TILE)
            o_ref[:, sl] += _dot(act, w2_ref[:valid, sl])

    last_valid = D_FF - (nf - 1) * tf
    if last_valid == tf:
        ffn_step(tf)
    else:
        pl.when(j < nf - 1)(lambda: ffn_step(tf))
        pl.when(j == nf - 1)(lambda: ffn_step(last_valid))


def _ffn(x2, norm, w1, w3, w2, layer, tm):
    T, D = x2.shape
    tf = FF_TILE
    nf = -(-D_FF // tf)
    return pl.pallas_call(
        functools.partial(_ffn_kernel, nf=nf),
        grid=(T // tm, nf),
        in_specs=[
            _x_spec(tm, D, T // tm),
            pl.BlockSpec((None, 1, D), lambda i, j: (layer, 0, 0)),
            pl.BlockSpec((None, D, tf), lambda i, j: (layer, 0, j)),
            pl.BlockSpec((None, D, tf), lambda i, j: (layer, 0, j)),
            pl.BlockSpec((None, tf, D), lambda i, j: (layer, j, 0)),
        ],
        out_specs=pl.BlockSpec((tm, D), lambda i, j: (i, 0)),
        out_shape=jax.ShapeDtypeStruct((T, D), _F32),
        scratch_shapes=[pltpu.VMEM((tm, D), _BF)],
        compiler_params=_params("parallel", "arbitrary"),
        name="ffn",
    )(x2, norm, w1, w3, w2)


def _ple_kernel(x_ref, g_ref, p_ref, wg_ref, wp_ref, o_ref, h_ref):
    j = pl.program_id(1)

    @pl.when(j == 0)
    def _():
        h_ref[...] = _rms(x_ref[...], g_ref[...]).astype(_BF)

    gate = jax.nn.sigmoid(_dot(h_ref[...], wg_ref[...]))
    tn = o_ref.shape[1]
    cols = pl.ds(pl.multiple_of(j * tn, tn), tn)
    o_ref[...] = x_ref[:, cols] + gate * _dot(p_ref[...].astype(_BF), wp_ref[...])


def _ple(x2, norm, p, w_gate, w_proj, layer, tm, tn):
    T, D = x2.shape
    return pl.pallas_call(
        _ple_kernel,
        grid=(T // tm, D // tn),
        in_specs=[
            pl.BlockSpec((tm, D), lambda i, j: (i, 0)),
            pl.BlockSpec((None, 1, D), lambda i, j: (layer, 0, 0)),
            pl.BlockSpec((None, tm, PLE_DIM), lambda i, j: (layer, i, 0)),
            pl.BlockSpec((None, D, tn), lambda i, j: (layer, 0, j)),
            pl.BlockSpec((None, PLE_DIM, tn), lambda i, j: (layer, 0, j)),
        ],
        out_specs=pl.BlockSpec((tm, tn), lambda i, j: (i, j)),
        out_shape=jax.ShapeDtypeStruct((T, D), _F32),
        scratch_shapes=[pltpu.VMEM((tm, D), _BF)],
        compiler_params=_params("parallel", "arbitrary"),
        name="ple",
    )(x2, norm, p, w_gate, w_proj)


def _slab_of_residue(r):
    return (r % 4) * 4 + r // 4


def _proj_kernel(x_ref, g_ref, w_ref, gain_ref, nat_ref, qkv_ref, h_ref, hp_ref, *, n_nat, n_norm):
    j = pl.program_id(1)
    tm = x_ref.shape[0]
    nsub = tm // PERM_ROWS

    @pl.when(j == 0)
    def _():
        h = _rms(x_ref[...], g_ref[...]).astype(_BF)
        h_ref[...] = h
        lam = lax.broadcasted_iota(jnp.int32, (PERM_ROWS, PERM_ROWS), 0)
        tau = lax.broadcasted_iota(jnp.int32, (PERM_ROWS, PERM_ROWS), 1)
        r = tau % ATTN_SLABS
        perm = (lam == _slab_of_residue(r) * (PERM_ROWS // ATTN_SLABS) + tau // ATTN_SLABS).astype(_BF)
        for s in range(nsub):
            rows = slice(s * PERM_ROWS, (s + 1) * PERM_ROWS)
            hp_ref[rows, :] = _dot(perm, h[rows, :]).astype(_BF)

    @pl.when(j < n_nat)
    def _():
        nat_ref[...] = _dot(h_ref[...], w_ref[...])

    def store_qkv(cols, y):
        run = PERM_ROWS // ATTN_SLABS
        for s in range(nsub):
            for slab in range(ATTN_SLABS):
                r0 = s * PERM_ROWS + slab * run
                qkv_ref[slab, s * run:(s + 1) * run, cols] = y[r0:r0 + run, :]

    is_qk = jnp.logical_and(j >= n_nat, j < n_nat + n_norm)

    @pl.when(is_qk)
    def _():
        acc = _dot(hp_ref[...], w_ref[...])
        for c in range(acc.shape[1] // HEAD_DIM):
            sl = slice(c * HEAD_DIM, (c + 1) * HEAD_DIM)
            store_qkv(sl, _rms(acc[:, sl], gain_ref[:, sl]))

    @pl.when(j >= n_nat + n_norm)
    def _():
        store_qkv(slice(None), _dot(hp_ref[...], w_ref[...]))


def _proj(x2, norm, w_in, gain, layer, tm, tn):
    T, D = x2.shape
    n_zd = (COL_GATES - COL_ZD) // tn
    n_nat = n_zd + COL_Q // tn
    n_norm = (COL_V - COL_Q) // tn
    n_all = MIX_COLS // tn
    assert tm % PERM_ROWS == 0 and ATTN_TILE % tm == 0
    per_tile = ATTN_TILE // tm

    def w_col(j):
        return jnp.where(j < n_zd, j + COL_ZD // tn, j - n_zd)

    return pl.pallas_call(
        functools.partial(_proj_kernel, n_nat=n_nat, n_norm=n_norm),
        grid=(T // tm, n_all),
        in_specs=[
            _x_spec(tm, D, T // tm),
            pl.BlockSpec((None, 1, D), lambda i, j: (layer, 0, 0)),
            pl.BlockSpec((None, D, tn), lambda i, j: (layer, 0, w_col(j))),
            pl.BlockSpec((None, 1, tn), lambda i, j: (layer, 0, w_col(j))),
        ],
        out_specs=[
            pl.BlockSpec((tm, tn), lambda i, j: (i, jnp.minimum(j, n_nat - 1))),
            pl.BlockSpec((None, ATTN_SLABS, tm // ATTN_SLABS, tn),
                         lambda i, j: (i // per_tile, 0, i % per_tile, jnp.maximum(j - n_nat, 0))),
        ],
        out_shape=[jax.ShapeDtypeStruct((T, NAT_COLS), _F32),
                   jax.ShapeDtypeStruct((T // ATTN_TILE, ATTN_SLABS, ATTN_BLOCK, QKV_COLS), _F32)],
        scratch_shapes=[pltpu.VMEM((tm, D), _BF), pltpu.VMEM((tm, D), _BF)],
        compiler_params=_params("parallel", "arbitrary"),
        name="proj",
    )(x2, norm, w_in, gain)


def _rglru_kernel(xa_ref, halo_ref, ga_ref, cw_ref, cb_ref, wa_ref, ba_ref, wx_ref, bx_ref,
                  lam_ref, o_ref, carry_ref, sa_ref, sb_ref, xe_ref, *, tb):
    i = pl.program_id(1)

    @pl.when(i == 0)
    def _():
        carry_ref[...] = jnp.zeros_like(carry_ref)

    xe_ref[:CONV_HALO, :] = jnp.where(i == 0, 0.0, halo_ref[...])
    xe_ref[CONV_HALO:, :] = xa_ref[...]
    cw = cw_ref[...]
    xc = cb_ref[...] + cw[0:1] * xa_ref[...]
    for j in range(1, CONV_WIDTH):
        xc = xc + cw[j:j + 1] * xe_ref[pl.ds(CONV_HALO - j, tb), :]
    xcb = xc.astype(_BF)
    r_lin, i_lin = [], []
    for p in range(WIDTH // RNN_BLOCK):
        sl = slice(p * RNN_BLOCK, (p + 1) * RNN_BLOCK)
        r_lin.append(_dot(xcb[:, sl], wa_ref[p]))
        i_lin.append(_dot(xcb[:, sl], wx_ref[p]))
    r = jax.nn.sigmoid(jnp.concatenate(r_lin, axis=1) + ba_ref[...])
    ig = jax.nn.sigmoid(jnp.concatenate(i_lin, axis=1) + bx_ref[...])
    z = -lam_ref[...]
    softplus = jnp.maximum(z, 0.0) + jnp.log1p(jnp.exp(-jnp.abs(z)))
    log_a = (-RG_C * r) * softplus
    a = jnp.exp(log_a)
    y = 1.0 - a * a
    b = jnp.where(y > 0.0, y * lax.rsqrt(y), 0.0) * (ig * xc)
    ng = tb // SCAN_GROUP
    a = a.reshape(ng, SCAN_GROUP, WIDTH)
    b = b.reshape(ng, SCAN_GROUP, WIDTH)
    row = lax.broadcasted_iota(jnp.int32, (1, SCAN_GROUP, 1), 1)
    s = 1
    while s < SCAN_GROUP:
        keep = row >= s
        a_prev = jnp.where(keep, pltpu.roll(a, s, axis=1), 1.0)
        b_prev = jnp.where(keep, pltpu.roll(b, s, axis=1), 0.0)
        b = a * b_prev + b
        a = a * a_prev
        s *= 2
    sa_ref[...] = a
    sb_ref[...] = b

    def chain(g, carry):
        hg = sa_ref[g] * carry + sb_ref[g]
        sb_ref[g] = hg
        return hg[SCAN_GROUP - 1:SCAN_GROUP, :]

    carry_ref[0:1, :] = lax.fori_loop(0, ng, chain, carry_ref[0:1, :], unroll=8)
    h = sb_ref[...].reshape(tb, WIDTH)
    o_ref[...] = (h * jax.nn.gelu(ga_ref[...])).astype(_BF)


def _rglru(proj, cw, cb, wa, ba, wx, bx, lam, layer, B, S, tb):
    nb = S // tb
    hb = tb // CONV_HALO
    vec = pl.BlockSpec((None, 1, WIDTH), lambda b, i: (layer, 0, 0))
    mat = pl.BlockSpec((None, WIDTH // RNN_BLOCK, RNN_BLOCK, RNN_BLOCK), lambda b, i: (layer, 0, 0, 0))
    return pl.pallas_call(
        functools.partial(_rglru_kernel, tb=tb),
        grid=(B, nb),
        in_specs=[
            pl.BlockSpec((tb, WIDTH), lambda b, i: (b * nb + i, NAT_XA // WIDTH)),
            pl.BlockSpec((CONV_HALO, WIDTH),
                         lambda b, i: (jnp.maximum((b * nb + i) * hb - 1, 0), NAT_XA // WIDTH)),
            pl.BlockSpec((tb, WIDTH), lambda b, i: (b * nb + i, NAT_GA // WIDTH)),
            pl.BlockSpec((None, CONV_WIDTH, WIDTH), lambda b, i: (layer, 0, 0)),
            vec, mat, vec, mat, vec, vec,
        ],
        out_specs=pl.BlockSpec((tb, WIDTH), lambda b, i: (b * nb + i, 0)),
        out_shape=jax.ShapeDtypeStruct((B * S, WIDTH), _BF),
        scratch_shapes=[pltpu.VMEM((8, WIDTH), _F32),
                        pltpu.VMEM((tb // SCAN_GROUP, SCAN_GROUP, WIDTH), _F32),
                        pltpu.VMEM((tb // SCAN_GROUP, SCAN_GROUP, WIDTH), _F32),
                        pltpu.VMEM((tb + CONV_HALO, WIDTH), _F32)],
        compiler_params=_params("parallel", "arbitrary"),
        name="rglru",
    )(proj, proj, proj, cw, cb, wa, ba, wx, bx, lam)


def _pool_kernel(xb_ref, halo_ref, pw_ref, sc_ref, o_ref, *, tb):
    i = pl.program_id(1)
    x = xb_ref[...]
    halo = jnp.where(i == 0, 0.0, halo_ref[...])
    xe = jnp.concatenate([halo, x], axis=0)
    pos = i * tb + lax.broadcasted_iota(jnp.int32, (tb, 1), 0)
    for g, win in enumerate(POOL_WINDOWS):
        sl = slice(g * POOL_GROUP, (g + 1) * POOL_GROUP)
        s = xe[:, sl]
        sh = 1
        while sh < win:
            s = s + pltpu.roll(s, sh, axis=0)
            sh *= 2
        cnt = jnp.minimum(pos + 1, win).astype(_F32)
        pooled = s[POOL_HALO:] / cnt - x[:, sl]
        y = _dot(pooled.astype(_BF), pw_ref[g])
        o_ref[:, sl] = (y * sc_ref[:, sl]).astype(_BF)


def _pool(proj, pw, sc, layer, B, S, tb):
    nb = S // tb
    hb = tb // POOL_HALO
    return pl.pallas_call(
        functools.partial(_pool_kernel, tb=tb),
        grid=(B, nb),
        in_specs=[
            pl.BlockSpec((tb, WIDTH), lambda b, i: (b * nb + i, NAT_XB // WIDTH)),
            pl.BlockSpec((POOL_HALO, WIDTH),
                         lambda b, i: (jnp.maximum((b * nb + i) * hb - 1, 0), NAT_XB // WIDTH)),
            pl.BlockSpec((None, len(POOL_WINDOWS), POOL_GROUP, POOL_GROUP), lambda b, i: (layer, 0, 0, 0)),
            pl.BlockSpec((None, 1, WIDTH), lambda b, i: (layer, 0, 0)),
        ],
        out_specs=pl.BlockSpec((tb, WIDTH), lambda b, i: (b * nb + i, 0)),
        out_shape=jax.ShapeDtypeStruct((B * S, WIDTH), _BF),
        compiler_params=_params("parallel", "parallel"),
        name="pool",
    )(proj, proj, pw, sc)


def _attn_order(g):
    slab_res = np.empty(ATTN_SLABS, np.int64)
    for r in range(ATTN_SLABS):
        slab_res[_slab_of_residue(r)] = r
    if g == 0:
        qi = (slab_res[:, None] + ATTN_SLABS * np.arange(8)[None, :]).reshape(-1)
        ki = (slab_res[:, None] + ATTN_SLABS * np.arange(16)[None, :]).reshape(-1)
    elif g == 1:
        qi = (np.arange(4)[:, None] + 4 * np.arange(32)[None, :]).reshape(-1)
        ki = (np.arange(4)[:, None] + 4 * np.arange(64)[None, :]).reshape(-1)
    else:
        qi = np.arange(ATTN_BLOCK)
        ki = np.arange(2 * ATTN_BLOCK)
    return qi, ki


def _attn_bias():
    out = np.empty((2 * len(ATTN_DILATIONS), ATTN_BLOCK, 2 * ATTN_BLOCK), np.float32)
    for g in range(len(ATTN_DILATIONS)):
        qi, ki = _attn_order(g)
        band = (ki[None, :] >= qi[:, None]) & (ki[None, :] <= qi[:, None] + ATTN_BLOCK)
        out[2 * g] = np.where(band, 0.0, NEG_INF)
        out[2 * g + 1] = np.where(band & (ki[None, :] >= ATTN_BLOCK), 0.0, NEG_INF)
    return out


def _attn_kernel(q0_ref, q1_ref, q2_ref, kc_ref, kp_ref, vc_ref, vp_ref, bias_ref, o_ref,
                 ks_ref, vs_ref, o0_ref, o1_ref, o2_ref, l0_ref, l1_ref, l2_ref):
    n = pl.program_id(2)
    NJ = ATTN_BLOCK
    ks_ref[:, NJ:, :] = kc_ref[...]
    vs_ref[:, NJ:, :] = vc_ref[...]

    @pl.when(n == 0)
    def _():
        ks_ref[:, :NJ, :] = jnp.zeros((ATTN_SLABS, NJ, HEAD_DIM), _F32)
        vs_ref[:, :NJ, :] = jnp.zeros((ATTN_SLABS, NJ, HEAD_DIM), _F32)

    @pl.when(n > 0)
    def _():
        ks_ref[:, :NJ, :] = kp_ref[...]
        vs_ref[:, :NJ, :] = vp_ref[...]

    scale = HEAD_DIM ** -0.5
    exp2_scale = scale * 1.4426950408889634

    shapes = ((ATTN_SLABS, 8), (4, 32), (1, ATTN_BLOCK))
    for g, (q_ref, og_ref, lg_ref) in enumerate(((q0_ref, o0_ref, l0_ref), (q1_ref, o1_ref, l1_ref),
                                                 (q2_ref, o2_ref, l2_ref))):
        ns, nj = shapes[g]
        classes = ATTN_SLABS // ns

        def body(idx, carry, g=g, q_ref=q_ref, og_ref=og_ref, lg_ref=lg_ref, ns=ns, nj=nj, classes=classes):
            c = idx % classes
            m = idx // classes
            slabs = pl.ds(c * ns, ns)
            j0 = pl.multiple_of(m * nj, 8)
            q = q_ref[slabs, pl.ds(j0, nj), :].reshape(ATTN_BLOCK, HEAD_DIM).astype(_BF)
            kj = pl.ds(pl.multiple_of(NJ - nj + m * nj, 8), 2 * nj)
            k = ks_ref[slabs, kj, :].reshape(2 * ATTN_BLOCK, HEAD_DIM).astype(_BF)
            v = vs_ref[slabs, kj, :].reshape(2 * ATTN_BLOCK, HEAD_DIM).astype(_BF)
            first = jnp.logical_and(n == 0, m == 0).astype(jnp.int32)
            s = (lax.dot_general(q, k, (((1,), (1,)), ((), ())), preferred_element_type=_F32)
                 + bias_ref[2 * g + first])
            mx = jnp.max(s, axis=-1, keepdims=True)
            p = jnp.exp2((s - mx) * exp2_scale)
            l = jnp.sum(p, axis=-1, keepdims=True)
            o = _dot(p.astype(_BF), v) / l
            lse = mx * scale + jnp.log(l)
            og_ref[slabs, pl.ds(j0, nj), :] = o.reshape(ns, nj, HEAD_DIM)
            lg_ref[slabs, pl.ds(j0, nj), :] = jnp.broadcast_to(lse, (ATTN_BLOCK, HEAD_DIM)).reshape(ns, nj, HEAD_DIM)
            return carry

        lax.fori_loop(0, ATTN_TILE // ATTN_BLOCK, body, 0, unroll=ATTN_UNROLL)

    l0, l1, l2 = l0_ref[...], l1_ref[...], l2_ref[...]
    mx = jnp.maximum(jnp.maximum(l0, l1), l2)
    w0, w1, w2 = jnp.exp(l0 - mx), jnp.exp(l1 - mx), jnp.exp(l2 - mx)
    out = (w0 * o0_ref[...] + w1 * o1_ref[...] + w2 * o2_ref[...]) / (w0 + w1 + w2)
    for r in range(ATTN_SLABS):
        o_ref[pl.ds(r, NJ, stride=ATTN_SLABS), :] = out[_slab_of_residue(r)]


def _attn(qkv, B, S):
    nt = S // ATTN_TILE
    qb, kb, vb = QKV_Q // HEAD_DIM, QKV_K // HEAD_DIM, QKV_V // HEAD_DIM
    blk = (None, ATTN_SLABS, ATTN_BLOCK, HEAD_DIM)

    def cur(col0):
        return pl.BlockSpec(blk, lambda b, h, n: (b * nt + n, 0, 0, col0 + h))

    def prev(col0):
        return pl.BlockSpec(blk, lambda b, h, n: (b * nt + jnp.maximum(n - 1, 0), 0, 0, col0 + h))

    n_bias = 2 * len(ATTN_DILATIONS)
    big = pltpu.VMEM((ATTN_SLABS, 2 * ATTN_BLOCK, HEAD_DIM), _F32)
    tile = pltpu.VMEM((ATTN_SLABS, ATTN_BLOCK, HEAD_DIM), _F32)
    return pl.pallas_call(
        _attn_kernel,
        grid=(B, KV_HEADS, nt),
        in_specs=[cur(qb), cur(qb + KV_HEADS), cur(qb + 2 * KV_HEADS),
                  cur(kb), prev(kb), cur(vb), prev(vb),
                  pl.BlockSpec((n_bias, ATTN_BLOCK, 2 * ATTN_BLOCK), lambda b, h, n: (0, 0, 0))],
        out_specs=pl.BlockSpec((ATTN_TILE, HEAD_DIM), lambda b, h, n: (b * nt + n, h)),
        out_shape=jax.ShapeDtypeStruct((B * S, WIDTH), _F32),
        scratch_shapes=[big, big, tile, tile, tile, tile, tile, tile],
        compiler_params=_params("parallel", "parallel", "parallel"),
        name="attn",
    )(qkv, qkv, qkv, qkv, qkv, qkv, qkv, jnp.asarray(_attn_bias()))


def _sgu_kernel(z_ref, g_ref, ws_ref, b_ref, o_ref, *, tb):
    gz = jax.nn.gelu(z_ref[...])
    u = gz[:, :WIDTH]
    vv = _rms(gz[:, WIDTH:], g_ref[...]).astype(_BF)
    row = lax.broadcasted_iota(jnp.int32, (SG_CHUNK, SG_CHUNK), 0)
    col = lax.broadcasted_iota(jnp.int32, (SG_CHUNK, SG_CHUNK), 1)
    tri = row >= col
    for g in range(SG_GROUPS):
        w = jnp.where(tri, ws_ref[g], 0.0).astype(_BF)
        cs = slice(g * SG_CHUNK, (g + 1) * SG_CHUNK)
        for c in range(tb // SG_CHUNK):
            rs = slice(c * SG_CHUNK, (c + 1) * SG_CHUNK)
            mixed = _dot(w, vv[rs, cs]) + b_ref[g]
            o_ref[rs, cs] = (u[rs, cs] * mixed).astype(_BF)


def _sgu(proj, sg_norm, sg_w, sg_b, layer, T, tb):
    return pl.pallas_call(
        functools.partial(_sgu_kernel, tb=tb),
        grid=(T // tb,),
        in_specs=[
            pl.BlockSpec((tb, 2 * WIDTH), lambda i: (i, NAT_ZD // (2 * WIDTH))),
            pl.BlockSpec((None, 1, WIDTH), lambda i: (layer, 0, 0)),
            pl.BlockSpec((None, SG_GROUPS, SG_CHUNK, SG_CHUNK), lambda i: (layer, 0, 0, 0)),
            pl.BlockSpec((None, SG_GROUPS, SG_CHUNK, 1), lambda i: (layer, 0, 0, 0)),
        ],
        out_specs=pl.BlockSpec((tb, WIDTH), lambda i: (i, 0)),
        out_shape=jax.ShapeDtypeStruct((T, WIDTH), _BF),
        compiler_params=_params("parallel"),
        name="sgu",
    )(proj, sg_norm, sg_w, sg_b)


def _merge_kernel(x_ref, g_ref, ya_ref, yb_ref, yc_ref, yd_ref, wg0_ref, wg1_ref, wg2_ref, wg3_ref,
                  bg_ref, wb_ref, wo_ref, o_ref, h_ref):
    @pl.when(pl.program_id(1) == 0)
    def _():
        x = x_ref[...]
        h_ref[...] = _rms(x, g_ref[...]).astype(_BF)
        o_ref[...] = x

    h = h_ref[...]
    merged = None
    for b, (y_ref, wg_ref) in enumerate(((ya_ref, wg0_ref), (yb_ref, wg1_ref),
                                         (yc_ref, wg2_ref), (yd_ref, wg3_ref))):
        gate = jax.nn.sigmoid(_dot(h, wg_ref[...]) + bg_ref[b:b + 1, :])
        term = gate * _dot(y_ref[...].astype(_BF), wb_ref[b])
        merged = term if merged is None else merged + term
    o_ref[...] += _dot(merged.astype(_BF), wo_ref[...])


def _merge(x2, norm, ys, w_in, b_gate, w_branch, w_out, layer, tm, tn):
    T, D = x2.shape
    y_spec = pl.BlockSpec((tm, WIDTH), lambda i, j: (i, 0))

    def gate_spec(b):
        col0 = (COL_GATES + b * D) // tn
        return pl.BlockSpec((None, D, tn), lambda i, j: (layer, 0, col0 + j))

    return pl.pallas_call(
        _merge_kernel,
        grid=(T // tm, D // tn),
        in_specs=[
            _x_spec(tm, D, T // tm),
            pl.BlockSpec((None, 1, D), lambda i, j: (layer, 0, 0)),
            y_spec, y_spec, y_spec, y_spec,
            gate_spec(0), gate_spec(1), gate_spec(2), gate_spec(3),
            pl.BlockSpec((None, N_BRANCH, tn), lambda i, j: (layer, 0, j)),
            pl.BlockSpec((None, N_BRANCH, WIDTH, tn), lambda i, j: (layer, 0, 0, j)),
            pl.BlockSpec((None, tn, D), lambda i, j: (layer, j, 0)),
        ],
        out_specs=pl.BlockSpec((tm, D), lambda i, j: (i, 0)),
        out_shape=jax.ShapeDtypeStruct((T, D), _F32),
        scratch_shapes=[pltpu.VMEM((tm, D), _BF)],
        compiler_params=_params("parallel", "arbitrary"),
        name="merge",
    )(x2, norm, *ys, w_in, w_in, w_in, w_in, b_gate, w_branch, w_out)


def _block_diag(w):
    L = w.shape[0]
    per = RNN_BLOCK // RNN_HEAD_DIM
    w = w.reshape(L, RNN_HEADS // per, per, RNN_HEAD_DIM, RNN_HEAD_DIM)
    eye = jnp.eye(per, dtype=w.dtype)
    bd = jnp.einsum('lphij,hk->lphikj', w, eye)
    return bd.reshape(L, RNN_HEADS // per, RNN_BLOCK, RNN_BLOCK).astype(_BF)


def kernel(x, p, ffn1_norm, ffn1_w1, ffn1_w3, ffn1_w2, mix_norm, w_in, b_gate, conv_w, conv_b, rg_wa, rg_ba, rg_wx, rg_bx, rg_lambda, pool_w, pool_scale, q_gain, k_gain, sg_norm, sg_w, sg_b, w_branch, w_out, ffn2_norm, ffn2_w1, ffn2_w3, ffn2_w2, ple_norm, ple_gate_w, ple_proj):
    B, S, D = x.shape
    L = w_in.shape[0]
    T = B * S
    assert D == D_MODEL and S % ATTN_TILE == 0
    tm = 512

    def vec(a):
        return a.reshape(L, 1, a.shape[-1])

    f1 = (ffn1_w1.astype(_BF), ffn1_w3.astype(_BF), ffn1_w2.astype(_BF))
    f2 = (ffn2_w1.astype(_BF), ffn2_w3.astype(_BF), ffn2_w2.astype(_BF))
    w_in_b = w_in.astype(_BF)
    w_branch_b = w_branch.astype(_BF)
    w_out_b = w_out.astype(_BF)
    ple_gate_b = ple_gate_w.astype(_BF)
    ple_proj_b = ple_proj.astype(_BF)
    pool_w_b = pool_w.astype(_BF)
    wa_bd, wx_bd = _block_diag(rg_wa), _block_diag(rg_wx)
    ones = jnp.ones((L, COL_Q), _F32)
    qk_gain = jnp.concatenate(
        [ones, jnp.tile(q_gain, (1, (COL_K - COL_Q) // HEAD_DIM)),
         jnp.tile(k_gain, (1, (COL_V - COL_K) // HEAD_DIM)),
         jnp.ones((L, MIX_COLS - COL_V), _F32)], axis=1).reshape(L, 1, MIX_COLS)
    p2 = p.reshape(L, T, PLE_DIM)
    sg_b4 = sg_b.reshape(L, SG_GROUPS, SG_CHUNK, 1)

    x2 = x.reshape(T, D)
    for i in range(L):
        x2 = _ffn(x2, vec(ffn1_norm), *f1, i, 1024)
        proj, qkv = _proj(x2, vec(mix_norm), w_in_b, qk_gain, i, 1024, 1024)
        ya = _rglru(proj, conv_w, vec(conv_b), wa_bd, vec(rg_ba), wx_bd, vec(rg_bx), vec(rg_lambda),
                    i, B, S, 256)
        yb = _pool(proj, pool_w_b, vec(pool_scale), i, B, S, 512)
        yc = _attn(qkv, B, S)
        yd = _sgu(proj, vec(sg_norm), sg_w, sg_b4, i, T, 512)
        x2 = _merge(x2, vec(mix_norm), (ya, yb, yc, yd), w_in_b, b_gate, w_branch_b, w_out_b, i, tm, 256)
        x2 = _ffn(x2, vec(ffn2_norm), *f2, i, 1024)
        x2 = _ple(x2, vec(ple_norm), p2, ple_gate_b, ple_proj_b, i, 1024, PLE_TILE)
    return x2.reshape(B, S, D)
```

```python
import functools

import numpy as np

import jax
import jax.numpy as jnp
from jax import lax
from jax.experimental import pallas as pl
from jax.experimental.pallas import tpu as pltpu

EPS = 1e-6
NEG_INF = -1e30
D_MODEL = 2048
D_FF = 5504
FF_TILE = 512
PLE_TILE = 1024
PLE_DIM = 256
WIDTH = 1024
RNN_HEADS = 16
RNN_HEAD_DIM = 64
RNN_BLOCK = 256
RG_C = 8.0
CONV_WIDTH = 4
POOL_WINDOWS = (2, 4, 8, 16)
POOL_GROUP = 256
POOL_HALO = 16
CONV_HALO = 8
SCAN_GROUP = 8
HEAD_DIM = 128
KV_HEADS = 8
ATTN_DILATIONS = (1, 4, 16)
ATTN_BLOCK = 128
ATTN_TILE = ATTN_BLOCK * max(ATTN_DILATIONS)
ATTN_UNROLL = 16
SG_CHUNK = 128
SG_GROUPS = 8
N_BRANCH = 4
COL_XA, COL_GA, COL_XB, COL_Q, COL_K, COL_V, COL_ZD, COL_GATES = (
    0, 1024, 2048, 3072, 6144, 7168, 8192, 10240)
MIX_COLS = COL_GATES
NAT_ZD, NAT_XA, NAT_GA, NAT_XB, NAT_COLS = 0, 2048, 3072, 4096, 5120
QKV_Q, QKV_K, QKV_V, QKV_COLS = 0, 3072, 4096, 5120
ATTN_SLABS = 16
PERM_ROWS = 256
VMEM_LIMIT = 56 * 1024 * 1024

_BF = jnp.bfloat16
_F32 = jnp.float32


def _params(*sem):
    return pltpu.CompilerParams(dimension_semantics=sem, vmem_limit_bytes=VMEM_LIMIT)


def _rms(x, g):
    return x * lax.rsqrt(jnp.mean(x * x, axis=-1, keepdims=True) + EPS) * g


def _dot(a, b):
    return jnp.dot(a, b, preferred_element_type=_F32)


def _x_spec(tm, d, n_tiles):
    last = n_tiles - 1
    return pl.BlockSpec((tm, d), lambda i, j: (jnp.where(j == 0, i, jnp.minimum(i + 1, last)), 0))


def _ffn_kernel(x_ref, g_ref, w1_ref, w3_ref, w2_ref, o_ref, h_ref, *, nf):
    j = pl.program_id(1)
    tf = w1_ref.shape[1]

    @pl.when(j == 0)
    def _():
        h_ref[...] = _rms(x_ref[...], g_ref[...]).astype(_BF)

    def ffn_step(valid, first):
        h = h_ref[...]
        a = _dot(h, w1_ref[:, :valid])
        b = _dot(h, w3_ref[:, :valid])
        act = (0.5 * (a * jax.nn.sigmoid(a)) * b).astype(_BF)
        for c in range(o_ref.shape[1] // FF_TILE):
            sl = slice(c * FF_TILE, (c + 1) * FF_TILE)
            base = x_ref[:, sl] if first else o_ref[:, sl]
            o_ref[:, sl] = base + _dot(act, w2_ref[:valid, sl])

    last_valid = D_FF - (nf - 1) * tf
    pl.when(j == 0)(lambda: ffn_step(tf, True))
    pl.when(jnp.logical_and(j > 0, j < nf - 1))(lambda: ffn_step(tf, False))
    pl.when(j == nf - 1)(lambda: ffn_step(last_valid, False))


def _ffn(x2, norm, w1, w3, w2, layer, tm):
    T, D = x2.shape
    tf = FF_TILE
    nf = -(-D_FF // tf)
    return pl.pallas_call(
        functools.partial(_ffn_kernel, nf=nf),
        grid=(T // tm, nf),
        in_specs=[
            _x_spec(tm, D, T // tm),
            pl.BlockSpec((None, 1, D), lambda i, j: (layer, 0, 0)),
            pl.BlockSpec((None, D, tf), lambda i, j: (layer, 0, j)),
            pl.BlockSpec((None, D, tf), lambda i, j: (layer, 0, j)),
            pl.BlockSpec((None, tf, D), lambda i, j: (layer, j, 0)),
        ],
        out_specs=pl.BlockSpec((tm, D), lambda i, j: (i, 0)),
        out_shape=jax.ShapeDtypeStruct((T, D), _F32),
        scratch_shapes=[pltpu.VMEM((tm, D), _BF)],
        compiler_params=_params("parallel", "arbitrary"),
        name="ffn",
    )(x2, norm, w1, w3, w2)


def _ple_kernel(x_ref, g_ref, p_ref, wg_ref, wp_ref, o_ref, h_ref):
    j = pl.program_id(1)

    @pl.when(j == 0)
    def _():
        h_ref[...] = _rms(x_ref[...], g_ref[...]).astype(_BF)

    gate = jax.nn.sigmoid(_dot(h_ref[...], wg_ref[...]))
    tn = o_ref.shape[1]
    cols = pl.ds(pl.multiple_of(j * tn, tn), tn)
    o_ref[...] = x_ref[:, cols] + gate * _dot(p_ref[...].astype(_BF), wp_ref[...])


def _ple(x2, norm, p, w_gate, w_proj, layer, tm, tn):
    T, D = x2.shape
    return pl.pallas_call(
        _ple_kernel,
        grid=(T // tm, D // tn),
        in_specs=[
            pl.BlockSpec((tm, D), lambda i, j: (i, 0)),
            pl.BlockSpec((None, 1, D), lambda i, j: (layer, 0, 0)),
            pl.BlockSpec((None, tm, PLE_DIM), lambda i, j: (layer, i, 0)),
            pl.BlockSpec((None, D, tn), lambda i, j: (layer, 0, j)),
            pl.BlockSpec((None, PLE_DIM, tn), lambda i, j: (layer, 0, j)),
        ],
        out_specs=pl.BlockSpec((tm, tn), lambda i, j: (i, j)),
        out_shape=jax.ShapeDtypeStruct((T, D), _F32),
        scratch_shapes=[pltpu.VMEM((tm, D), _BF)],
        compiler_params=_params("parallel", "arbitrary"),
        name="ple",
    )(x2, norm, p, w_gate, w_proj)


def _slab_of_residue(r):
    return (r % 4) * 4 + r // 4


def _proj_kernel(x_ref, g_ref, w_ref, gain_ref, nat_ref, qkv_ref, h_ref, hp_ref, *, n_nat, n_norm):
    j = pl.program_id(1)
    tm = x_ref.shape[0]
    nsub = tm // PERM_ROWS

    @pl.when(j == 0)
    def _():
        h = _rms(x_ref[...], g_ref[...]).astype(_BF)
        h_ref[...] = h
        lam = lax.broadcasted_iota(jnp.int32, (PERM_ROWS, PERM_ROWS), 0)
        tau = lax.broadcasted_iota(jnp.int32, (PERM_ROWS, PERM_ROWS), 1)
        r = tau % ATTN_SLABS
        perm = (lam == _slab_of_residue(r) * (PERM_ROWS // ATTN_SLABS) + tau // ATTN_SLABS).astype(_BF)
        for s in range(nsub):
            rows = slice(s * PERM_ROWS, (s + 1) * PERM_ROWS)
            hp_ref[rows, :] = _dot(perm, h[rows, :]).astype(_BF)

    @pl.when(j < n_nat)
    def _():
        nat_ref[...] = _dot(h_ref[...], w_ref[...])

    def store_qkv(cols, y):
        run = PERM_ROWS // ATTN_SLABS
        for s in range(nsub):
            for slab in range(ATTN_SLABS):
                r0 = s * PERM_ROWS + slab * run
                qkv_ref[slab, s * run:(s + 1) * run, cols] = y[r0:r0 + run, :]

    is_qk = jnp.logical_and(j >= n_nat, j < n_nat + n_norm)

    @pl.when(is_qk)
    def _():
        acc = _dot(hp_ref[...], w_ref[...])
        for c in range(acc.shape[1] // HEAD_DIM):
            sl = slice(c * HEAD_DIM, (c + 1) * HEAD_DIM)
            store_qkv(sl, _rms(acc[:, sl], gain_ref[:, sl]))

    @pl.when(j >= n_nat + n_norm)
    def _():
        store_qkv(slice(None), _dot(hp_ref[...], w_ref[...]))


def _proj(x2, norm, w_in, gain, layer, tm, tn):
    T, D = x2.shape
    n_zd = (COL_GATES - COL_ZD) // tn
    n_nat = n_zd + COL_Q // tn
    n_norm = (COL_V - COL_Q) // tn
    n_all = MIX_COLS // tn
    assert tm % PERM_ROWS == 0 and ATTN_TILE % tm == 0
    per_tile = ATTN_TILE // tm

    def w_col(j):
        return jnp.where(j < n_zd, j + COL_ZD // tn, j - n_zd)

    return pl.pallas_call(
        functools.partial(_proj_kernel, n_nat=n_nat, n_norm=n_norm),
        grid=(T // tm, n_all),
        in_specs=[
            _x_spec(tm, D, T // tm),
            pl.BlockSpec((None, 1, D), lambda i, j: (layer, 0, 0)),
            pl.BlockSpec((None, D, tn), lambda i, j: (layer, 0, w_col(j))),
            pl.BlockSpec((None, 1, tn), lambda i, j: (layer, 0, w_col(j))),
        ],
        out_specs=[
            pl.BlockSpec((tm, tn), lambda i, j: (i, jnp.minimum(j, n_nat - 1))),
            pl.BlockSpec((None, ATTN_SLABS, tm // ATTN_SLABS, tn),
                         lambda i, j: (i // per_tile, 0, i % per_tile, jnp.maximum(j - n_nat, 0))),
        ],
        out_shape=[jax.ShapeDtypeStruct((T, NAT_COLS), _F32),
                   jax.ShapeDtypeStruct((T // ATTN_TILE, ATTN_SLABS, ATTN_BLOCK, QKV_COLS), _F32)],
        scratch_shapes=[pltpu.VMEM((tm, D), _BF), pltpu.VMEM((tm, D), _BF)],
        compiler_params=_params("parallel", "arbitrary"),
        name="proj",
    )(x2, norm, w_in, gain)


def _rglru_kernel(xa_ref, halo_ref, ga_ref, cw_ref, cb_ref, wa_ref, ba_ref, wx_ref, bx_ref,
                  lam_ref, o_ref, carry_ref, sa_ref, sb_ref, xe_ref, *, tb):
    i = pl.program_id(1)

    @pl.when(i == 0)
    def _():
        carry_ref[...] = jnp.zeros_like(carry_ref)

    xe_ref[:CONV_HALO, :] = jnp.where(i == 0, 0.0, halo_ref[...])
    xe_ref[CONV_HALO:, :] = xa_ref[...]
    cw = cw_ref[...]
    xc = cb_ref[...] + cw[0:1] * xa_ref[...]
    for j in range(1, CONV_WIDTH):
        xc = xc + cw[j:j + 1] * xe_ref[pl.ds(CONV_HALO - j, tb), :]
    xcb = xc.astype(_BF)
    r_lin, i_lin = [], []
    for p in range(WIDTH // RNN_BLOCK):
        sl = slice(p * RNN_BLOCK, (p + 1) * RNN_BLOCK)
        r_lin.append(_dot(xcb[:, sl], wa_ref[p]))
        i_lin.append(_dot(xcb[:, sl], wx_ref[p]))
    r = jax.nn.sigmoid(jnp.concatenate(r_lin, axis=1) + ba_ref[...])
    ig = jax.nn.sigmoid(jnp.concatenate(i_lin, axis=1) + bx_ref[...])
    z = -lam_ref[...]
    softplus = jnp.maximum(z, 0.0) + jnp.log1p(jnp.exp(-jnp.abs(z)))
    log_a = (-RG_C * r) * softplus
    a = jnp.exp(log_a)
    y = 1.0 - a * a
    b = jnp.where(y > 0.0, y * lax.rsqrt(y), 0.0) * (ig * xc)
    ng = tb // SCAN_GROUP
    a = a.reshape(ng, SCAN_GROUP, WIDTH)
    b = b.reshape(ng, SCAN_GROUP, WIDTH)
    row = lax.broadcasted_iota(jnp.int32, (1, SCAN_GROUP, 1), 1)
    s = 1
    while s < SCAN_GROUP:
        keep = row >= s
        a_prev = jnp.where(keep, pltpu.roll(a, s, axis=1), 1.0)
        b_prev = jnp.where(keep, pltpu.roll(b, s, axis=1), 0.0)
        b = a * b_prev + b
        a = a * a_prev
        s *= 2
    sa_ref[...] = a
    sb_ref[...] = b

    def chain(g, carry):
        hg = sa_ref[g] * carry + sb_ref[g]
        sb_ref[g] = hg
        return hg[SCAN_GROUP - 1:SCAN_GROUP, :]

    carry_ref[0:1, :] = lax.fori_loop(0, ng, chain, carry_ref[0:1, :], unroll=8)
    h = sb_ref[...].reshape(tb, WIDTH)
    o_ref[...] = (h * jax.nn.gelu(ga_ref[...])).astype(_BF)


def _rglru(proj, cw, cb, wa, ba, wx, bx, lam, layer, B, S, tb):
    nb = S // tb
    hb = tb // CONV_HALO
    vec = pl.BlockSpec((None, 1, WIDTH), lambda b, i: (layer, 0, 0))
    mat = pl.BlockSpec((None, WIDTH // RNN_BLOCK, RNN_BLOCK, RNN_BLOCK), lambda b, i: (layer, 0, 0, 0))
    return pl.pallas_call(
        functools.partial(_rglru_kernel, tb=tb),
        grid=(B, nb),
        in_specs=[
            pl.BlockSpec((tb, WIDTH), lambda b, i: (b * nb + i, NAT_XA // WIDTH)),
            pl.BlockSpec((CONV_HALO, WIDTH),
                         lambda b, i: (jnp.maximum((b * nb + i) * hb - 1, 0), NAT_XA // WIDTH)),
            pl.BlockSpec((tb, WIDTH), lambda b, i: (b * nb + i, NAT_GA // WIDTH)),
            pl.BlockSpec((None, CONV_WIDTH, WIDTH), lambda b, i: (layer, 0, 0)),
            vec, mat, vec, mat, vec, vec,
        ],
        out_specs=pl.BlockSpec((tb, WIDTH), lambda b, i: (b * nb + i, 0)),
        out_shape=jax.ShapeDtypeStruct((B * S, WIDTH), _BF),
        scratch_shapes=[pltpu.VMEM((8, WIDTH), _F32),
                        pltpu.VMEM((tb // SCAN_GROUP, SCAN_GROUP, WIDTH), _F32),
                        pltpu.VMEM((tb // SCAN_GROUP, SCAN_GROUP, WIDTH), _F32),
                        pltpu.VMEM((tb + CONV_HALO, WIDTH), _F32)],
        compiler_params=_params("parallel", "arbitrary"),
        name="rglru",
    )(proj, proj, proj, cw, cb, wa, ba, wx, bx, lam)


def _pool_kernel(xb_ref, halo_ref, pw_ref, sc_ref, o_ref, *, tb):
    i = pl.program_id(1)
    x = xb_ref[...]
    halo = jnp.where(i == 0, 0.0, halo_ref[...])
    xe = jnp.concatenate([halo, x], axis=0)
    pos = i * tb + lax.broadcasted_iota(jnp.int32, (tb, 1), 0)
    for g, win in enumerate(POOL_WINDOWS):
        sl = slice(g * POOL_GROUP, (g + 1) * POOL_GROUP)
        s = xe[:, sl]
        sh = 1
        while sh < win:
            s = s + pltpu.roll(s, sh, axis=0)
            sh *= 2
        cnt = jnp.minimum(pos + 1, win).astype(_F32)
        pooled = s[POOL_HALO:] / cnt - x[:, sl]
        y = _dot(pooled.astype(_BF), pw_ref[g])
        o_ref[:, sl] = (y * sc_ref[:, sl]).astype(_BF)


def _pool(proj, pw, sc, layer, B, S, tb):
    nb = S // tb
    hb = tb // POOL_HALO
    return pl.pallas_call(
        functools.partial(_pool_kernel, tb=tb),
        grid=(B, nb),
        in_specs=[
            pl.BlockSpec((tb, WIDTH), lambda b, i: (b * nb + i, NAT_XB // WIDTH)),
            pl.BlockSpec((POOL_HALO, WIDTH),
                         lambda b, i: (jnp.maximum((b * nb + i) * hb - 1, 0), NAT_XB // WIDTH)),
            pl.BlockSpec((None, len(POOL_WINDOWS), POOL_GROUP, POOL_GROUP), lambda b, i: (layer, 0, 0, 0)),
            pl.BlockSpec((None, 1, WIDTH), lambda b, i: (layer, 0, 0)),
        ],
        out_specs=pl.BlockSpec((tb, WIDTH), lambda b, i: (b * nb + i, 0)),
        out_shape=jax.ShapeDtypeStruct((B * S, WIDTH), _BF),
        compiler_params=_params("parallel", "parallel"),
        name="pool",
    )(proj, proj, pw, sc)


def _attn_order(g):
    slab_res = np.empty(ATTN_SLABS, np.int64)
    for r in range(ATTN_SLABS):
        slab_res[_slab_of_residue(r)] = r
    if g == 0:
        qi = (slab_res[:, None] + ATTN_SLABS * np.arange(8)[None, :]).reshape(-1)
        ki = (slab_res[:, None] + ATTN_SLABS * np.arange(16)[None, :]).reshape(-1)
    elif g == 1:
        qi = (np.arange(4)[:, None] + 4 * np.arange(32)[None, :]).reshape(-1)
        ki = (np.arange(4)[:, None] + 4 * np.arange(64)[None, :]).reshape(-1)
    else:
        qi = np.arange(ATTN_BLOCK)
        ki = np.arange(2 * ATTN_BLOCK)
    return qi, ki


def _attn_bias():
    out = np.empty((2 * len(ATTN_DILATIONS), ATTN_BLOCK, 2 * ATTN_BLOCK), np.float32)
    for g in range(len(ATTN_DILATIONS)):
        qi, ki = _attn_order(g)
        band = (ki[None, :] >= qi[:, None]) & (ki[None, :] <= qi[:, None] + ATTN_BLOCK)
        out[2 * g] = np.where(band, 0.0, NEG_INF)
        out[2 * g + 1] = np.where(band & (ki[None, :] >= ATTN_BLOCK), 0.0, NEG_INF)
    return out


def _attn_kernel(q0_ref, q1_ref, q2_ref, kc_ref, kp_ref, vc_ref, vp_ref, bias_ref, o_ref,
                 ks_ref, vs_ref, o0_ref, o1_ref, o2_ref, l0_ref, l1_ref, l2_ref):
    n = pl.program_id(2)
    NJ = ATTN_BLOCK
    ks_ref[:, NJ:, :] = kc_ref[...]
    vs_ref[:, NJ:, :] = vc_ref[...]

    @pl.when(n == 0)
    def _():
        ks_ref[:, :NJ, :] = jnp.zeros((ATTN_SLABS, NJ, HEAD_DIM), _F32)
        vs_ref[:, :NJ, :] = jnp.zeros((ATTN_SLABS, NJ, HEAD_DIM), _F32)

    @pl.when(n > 0)
    def _():
        ks_ref[:, :NJ, :] = kp_ref[...]
        vs_ref[:, :NJ, :] = vp_ref[...]

    scale = HEAD_DIM ** -0.5
    exp2_scale = scale * 1.4426950408889634

    shapes = ((ATTN_SLABS, 8), (4, 32), (1, ATTN_BLOCK))
    for g, (q_ref, og_ref, lg_ref) in enumerate(((q0_ref, o0_ref, l0_ref), (q1_ref, o1_ref, l1_ref),
                                                 (q2_ref, o2_ref, l2_ref))):
        ns, nj = shapes[g]
        classes = ATTN_SLABS // ns

        def body(idx, carry, g=g, q_ref=q_ref, og_ref=og_ref, lg_ref=lg_ref, ns=ns, nj=nj, classes=classes):
            c = idx % classes
            m = idx // classes
            slabs = pl.ds(c * ns, ns)
            j0 = pl.multiple_of(m * nj, 8)
            q = q_ref[slabs, pl.ds(j0, nj), :].reshape(ATTN_BLOCK, HEAD_DIM).astype(_BF)
            kj = pl.ds(pl.multiple_of(NJ - nj + m * nj, 8), 2 * nj)
            k = ks_ref[slabs, kj, :].reshape(2 * ATTN_BLOCK, HEAD_DIM).astype(_BF)
            v = vs_ref[slabs, kj, :].reshape(2 * ATTN_BLOCK, HEAD_DIM).astype(_BF)
            first = jnp.logical_and(n == 0, m == 0).astype(jnp.int32)
            s = (lax.dot_general(q, k, (((1,), (1,)), ((), ())), preferred_element_type=_F32)
                 + bias_ref[2 * g + first])
            mx = jnp.max(s, axis=-1, keepdims=True)
            p = jnp.exp2((s - mx) * exp2_scale)
            l = jnp.sum(p, axis=-1, keepdims=True)
            o = _dot(p.astype(_BF), v) / l
            lse = mx * scale + jnp.log(l)
            og_ref[slabs, pl.ds(j0, nj), :] = o.reshape(ns, nj, HEAD_DIM)
            lg_ref[slabs, pl.ds(j0, nj), :] = jnp.broadcast_to(lse, (ATTN_BLOCK, HEAD_DIM)).reshape(ns, nj, HEAD_DIM)
            return carry

        lax.fori_loop(0, ATTN_TILE // ATTN_BLOCK, body, 0, unroll=ATTN_UNROLL)

    l0, l1, l2 = l0_ref[...], l1_ref[...], l2_ref[...]
    mx = jnp.maximum(jnp.maximum(l0, l1), l2)
    w0, w1, w2 = jnp.exp(l0 - mx), jnp.exp(l1 - mx), jnp.exp(l2 - mx)
    out = (w0 * o0_ref[...] + w1 * o1_ref[...] + w2 * o2_ref[...]) / (w0 + w1 + w2)
    for r in range(ATTN_SLABS):
        o_ref[pl.ds(r, NJ, stride=ATTN_SLABS), :] = out[_slab_of_residue(r)]


def _attn(qkv, B, S):
    nt = S // ATTN_TILE
    qb, kb, vb = QKV_Q // HEAD_DIM, QKV_K // HEAD_DIM, QKV_V // HEAD_DIM
    blk = (None, ATTN_SLABS, ATTN_BLOCK, HEAD_DIM)

    def cur(col0):
        return pl.BlockSpec(blk, lambda b, h, n: (b * nt + n, 0, 0, col0 + h))

    def prev(col0):
        return pl.BlockSpec(blk, lambda b, h, n: (b * nt + jnp.maximum(n - 1, 0), 0, 0, col0 + h))

    n_bias = 2 * len(ATTN_DILATIONS)
    big = pltpu.VMEM((ATTN_SLABS, 2 * ATTN_BLOCK, HEAD_DIM), _F32)
    tile = pltpu.VMEM((ATTN_SLABS, ATTN_BLOCK, HEAD_DIM), _F32)
    return pl.pallas_call(
        _attn_kernel,
        grid=(B, KV_HEADS, nt),
        in_specs=[cur(qb), cur(qb + KV_HEADS), cur(qb + 2 * KV_HEADS),
                  cur(kb), prev(kb), cur(vb), prev(vb),
                  pl.BlockSpec((n_bias, ATTN_BLOCK, 2 * ATTN_BLOCK), lambda b, h, n: (0, 0, 0))],
        out_specs=pl.BlockSpec((ATTN_TILE, HEAD_DIM), lambda b, h, n: (b * nt + n, h)),
        out_shape=jax.ShapeDtypeStruct((B * S, WIDTH), _F32),
        scratch_shapes=[big, big, tile, tile, tile, tile, tile, tile],
        compiler_params=_params("parallel", "parallel", "parallel"),
        name="attn",
    )(qkv, qkv, qkv, qkv, qkv, qkv, qkv, jnp.asarray(_attn_bias()))


def _sgu_kernel(z_ref, g_ref, ws_ref, b_ref, o_ref, *, tb):
    gz = jax.nn.gelu(z_ref[...])
    u = gz[:, :WIDTH]
    vv = _rms(gz[:, WIDTH:], g_ref[...]).astype(_BF)
    row = lax.broadcasted_iota(jnp.int32, (SG_CHUNK, SG_CHUNK), 0)
    col = lax.broadcasted_iota(jnp.int32, (SG_CHUNK, SG_CHUNK), 1)
    tri = row >= col
    for g in range(SG_GROUPS):
        w = jnp.where(tri, ws_ref[g], 0.0).astype(_BF)
        cs = slice(g * SG_CHUNK, (g + 1) * SG_CHUNK)
        for c in range(tb // SG_CHUNK):
            rs = slice(c * SG_CHUNK, (c + 1) * SG_CHUNK)
            mixed = _dot(w, vv[rs, cs]) + b_ref[g]
            o_ref[rs, cs] = (u[rs, cs] * mixed).astype(_BF)


def _sgu(proj, sg_norm, sg_w, sg_b, layer, T, tb):
    return pl.pallas_call(
        functools.partial(_sgu_kernel, tb=tb),
        grid=(T // tb,),
        in_specs=[
            pl.BlockSpec((tb, 2 * WIDTH), lambda i: (i, NAT_ZD // (2 * WIDTH))),
            pl.BlockSpec((None, 1, WIDTH), lambda i: (layer, 0, 0)),
            pl.BlockSpec((None, SG_GROUPS, SG_CHUNK, SG_CHUNK), lambda i: (layer, 0, 0, 0)),
            pl.BlockSpec((None, SG_GROUPS, SG_CHUNK, 1), lambda i: (layer, 0, 0, 0)),
        ],
        out_specs=pl.BlockSpec((tb, WIDTH), lambda i: (i, 0)),
        out_shape=jax.ShapeDtypeStruct((T, WIDTH), _BF),
        compiler_params=_params("parallel"),
        name="sgu",
    )(proj, sg_norm, sg_w, sg_b)


def _merge_kernel(x_ref, g_ref, ya_ref, yb_ref, yc_ref, yd_ref, wg0_ref, wg1_ref, wg2_ref, wg3_ref,
                  bg_ref, wb_ref, wo_ref, o_ref, h_ref):
    j = pl.program_id(1)

    @pl.when(j == 0)
    def _():
        h_ref[...] = _rms(x_ref[...], g_ref[...]).astype(_BF)

    def step(first):
        h = h_ref[...]
        merged = None
        for b, (y_ref, wg_ref) in enumerate(((ya_ref, wg0_ref), (yb_ref, wg1_ref),
                                             (yc_ref, wg2_ref), (yd_ref, wg3_ref))):
            gate = jax.nn.sigmoid(_dot(h, wg_ref[...]) + bg_ref[b:b + 1, :])
            term = gate * _dot(y_ref[...].astype(_BF), wb_ref[b])
            merged = term if merged is None else merged + term
        base = x_ref[...] if first else o_ref[...]
        o_ref[...] = base + _dot(merged.astype(_BF), wo_ref[...])

    pl.when(j == 0)(lambda: step(True))
    pl.when(j > 0)(lambda: step(False))


def _merge(x2, norm, ys, w_in, b_gate, w_branch, w_out, layer, tm, tn):
    T, D = x2.shape
    y_spec = pl.BlockSpec((tm, WIDTH), lambda i, j: (i, 0))

    def gate_spec(b):
        col0 = (COL_GATES + b * D) // tn
        return pl.BlockSpec((None, D, tn), lambda i, j: (layer, 0, col0 + j))

    return pl.pallas_call(
        _merge_kernel,
        grid=(T // tm, D // tn),
        in_specs=[
            _x_spec(tm, D, T // tm),
            pl.BlockSpec((None, 1, D), lambda i, j: (layer, 0, 0)),
            y_spec, y_spec, y_spec, y_spec,
            gate_spec(0), gate_spec(1), gate_spec(2), gate_spec(3),
            pl.BlockSpec((None, N_BRANCH, tn), lambda i, j: (layer, 0, j)),
            pl.BlockSpec((None, N_BRANCH, WIDTH, tn), lambda i, j: (layer, 0, 0, j)),
            pl.BlockSpec((None, tn, D), lambda i, j: (layer, j, 0)),
        ],
        out_specs=pl.BlockSpec((tm, D), lambda i, j: (i, 0)),
        out_shape=jax.ShapeDtypeStruct((T, D), _F32),
        scratch_shapes=[pltpu.VMEM((tm, D), _BF)],
        compiler_params=_params("parallel", "arbitrary"),
        name="merge",
    )(x2, norm, *ys, w_in, w_in, w_in, w_in, b_gate, w_branch, w_out)


def _block_diag(w):
    L = w.shape[0]
    per = RNN_BLOCK // RNN_HEAD_DIM
    w = w.reshape(L, RNN_HEADS // per, per, RNN_HEAD_DIM, RNN_HEAD_DIM)
    eye = jnp.eye(per, dtype=w.dtype)
    bd = jnp.einsum('lphij,hk->lphikj', w, eye)
    return bd.reshape(L, RNN_HEADS // per, RNN_BLOCK, RNN_BLOCK).astype(_BF)


def kernel(x, p, ffn1_norm, ffn1_w1, ffn1_w3, ffn1_w2, mix_norm, w_in, b_gate, conv_w, conv_b, rg_wa, rg_ba, rg_wx, rg_bx, rg_lambda, pool_w, pool_scale, q_gain, k_gain, sg_norm, sg_w, sg_b, w_branch, w_out, ffn2_norm, ffn2_w1, ffn2_w3, ffn2_w2, ple_norm, ple_gate_w, ple_proj):
    B, S, D = x.shape
    L = w_in.shape[0]
    T = B * S
    assert D == D_MODEL and S % ATTN_TILE == 0
    tm = 512

    def vec(a):
        return a.reshape(L, 1, a.shape[-1])

    f1 = (ffn1_w1.astype(_BF), ffn1_w3.astype(_BF), ffn1_w2.astype(_BF))
    f2 = (ffn2_w1.astype(_BF), ffn2_w3.astype(_BF), ffn2_w2.astype(_BF))
    w_in_b = w_in.astype(_BF)
    w_branch_b = w_branch.astype(_BF)
    w_out_b = w_out.astype(_BF)
    ple_gate_b = ple_gate_w.astype(_BF)
    ple_proj_b = ple_proj.astype(_BF)
    pool_w_b = pool_w.astype(_BF)
    wa_bd, wx_bd = _block_diag(rg_wa), _block_diag(rg_wx)
    ones = jnp.ones((L, COL_Q), _F32)
    qk_gain = jnp.concatenate(
        [ones, jnp.tile(q_gain, (1, (COL_K - COL_Q) // HEAD_DIM)),
         jnp.tile(k_gain, (1, (COL_V - COL_K) // HEAD_DIM)),
         jnp.ones((L, MIX_COLS - COL_V), _F32)], axis=1).reshape(L, 1, MIX_COLS)
    p2 = p.reshape(L, T, PLE_DIM)
    sg_b4 = sg_b.reshape(L, SG_GROUPS, SG_CHUNK, 1)

    x2 = x.reshape(T, D)
    for i in range(L):
        x2 = _ffn(x2, vec(ffn1_norm), *f1, i, 1024)
        proj, qkv = _proj(x2, vec(mix_norm), w_in_b, qk_gain, i, 1024, 1024)
        ya = _rglru(proj, conv_w, vec(conv_b), wa_bd, vec(rg_ba), wx_bd, vec(rg_bx), vec(rg_lambda),
                    i, B, S, 256)
        yb = _pool(proj, pool_w_b, vec(pool_scale), i, B, S, 512)
        yc = _attn(qkv, B, S)
        yd = _sgu(proj, vec(sg_norm), sg_w, sg_b4, i, T, 512)
        x2 = _merge(x2, vec(mix_norm), (ya, yb, yc, yd), w_in_b, b_gate, w_branch_b, w_out_b, i, tm, 256)
        x2 = _ffn(x2, vec(ffn2_norm), *f2, i, 1024)
        x2 = _ple(x2, vec(ple_norm), p2, ple_gate_b, ple_proj_b, i, 1024, PLE_TILE)
    return x2.reshape(B, S, D)
```

```python
import functools

import numpy as np

import jax
import jax.numpy as jnp
from jax import lax
from jax.experimental import pallas as pl
from jax.experimental.pallas import tpu as pltpu

EPS = 1e-6
NEG_INF = -1e30
D_MODEL = 2048
D_FF = 5504
FF_TILE = 512
PLE_TILE = 1024
MERGE_TILE = 256
PROJ_TILE = 1024
PLE_DIM = 256
WIDTH = 1024
RNN_HEADS = 16
RNN_HEAD_DIM = 64
RNN_BLOCK = 256
RG_C = 8.0
CONV_WIDTH = 4
POOL_WINDOWS = (2, 4, 8, 16)
POOL_GROUP = 256
POOL_HALO = 16
CONV_HALO = 8
SCAN_GROUP = 8
HEAD_DIM = 128
KV_HEADS = 8
ATTN_DILATIONS = (1, 4, 16)
ATTN_BLOCK = 128
ATTN_TILE = ATTN_BLOCK * max(ATTN_DILATIONS)
ATTN_UNROLL = 16
SG_CHUNK = 128
SG_GROUPS = 8
N_BRANCH = 4
COL_XA, COL_GA, COL_XB, COL_Q, COL_K, COL_V, COL_ZD, COL_GATES = (
    0, 1024, 2048, 3072, 6144, 7168, 8192, 10240)
MIX_COLS = COL_GATES
NAT_ZD, NAT_XA, NAT_GA, NAT_XB, NAT_COLS = 0, 2048, 3072, 4096, 5120
QKV_Q, QKV_K, QKV_V, QKV_COLS = 0, 3072, 4096, 5120
ATTN_SLABS = 16
PERM_ROWS = 256
VMEM_LIMIT = 56 * 1024 * 1024

_BF = jnp.bfloat16
_F32 = jnp.float32


def _params(*sem):
    return pltpu.CompilerParams(dimension_semantics=sem, vmem_limit_bytes=VMEM_LIMIT)


def _rms(x, g):
    return x * lax.rsqrt(jnp.mean(x * x, axis=-1, keepdims=True) + EPS) * g


def _dot(a, b):
    return jnp.dot(a, b, preferred_element_type=_F32)


def _x_spec(tm, d, n_tiles):
    last = n_tiles - 1
    return pl.BlockSpec((tm, d), lambda i, j: (jnp.where(j == 0, i, jnp.minimum(i + 1, last)), 0))


def _ffn_kernel(x_ref, g_ref, w1_ref, w3_ref, w2_ref, o_ref, h_ref, *, nf):
    j = pl.program_id(1)
    tf = w1_ref.shape[1]

    @pl.when(j == 0)
    def _():
        h_ref[...] = _rms(x_ref[...], g_ref[...]).astype(_BF)

    def ffn_step(valid, first):
        h = h_ref[...]
        a = _dot(h, w1_ref[:, :valid])
        b = _dot(h, w3_ref[:, :valid])
        act = (0.5 * (a * jax.nn.sigmoid(a)) * b).astype(_BF)
        for c in range(o_ref.shape[1] // FF_TILE):
            sl = slice(c * FF_TILE, (c + 1) * FF_TILE)
            base = x_ref[:, sl] if first else o_ref[:, sl]
            o_ref[:, sl] = base + _dot(act, w2_ref[:valid, sl])

    last_valid = D_FF - (nf - 1) * tf
    pl.when(j == 0)(lambda: ffn_step(tf, True))
    pl.when(jnp.logical_and(j > 0, j < nf - 1))(lambda: ffn_step(tf, False))
    pl.when(j == nf - 1)(lambda: ffn_step(last_valid, False))


def _ffn(x2, norm, w1, w3, w2, layer, tm):
    T, D = x2.shape
    tf = FF_TILE
    nf = -(-D_FF // tf)
    return pl.pallas_call(
        functools.partial(_ffn_kernel, nf=nf),
        grid=(T // tm, nf),
        in_specs=[
            _x_spec(tm, D, T // tm),
            pl.BlockSpec((None, 1, D), lambda i, j: (layer, 0, 0)),
            pl.BlockSpec((None, D, tf), lambda i, j: (layer, 0, j)),
            pl.BlockSpec((None, D, tf), lambda i, j: (layer, 0, j)),
            pl.BlockSpec((None, tf, D), lambda i, j: (layer, j, 0)),
        ],
        out_specs=pl.BlockSpec((tm, D), lambda i, j: (i, 0)),
        out_shape=jax.ShapeDtypeStruct((T, D), _F32),
        scratch_shapes=[pltpu.VMEM((tm, D), _BF)],
        compiler_params=_params("parallel", "arbitrary"),
        name="ffn",
    )(x2, norm, w1, w3, w2)


def _ple_kernel(x_ref, g_ref, p_ref, wg_ref, wp_ref, o_ref, h_ref):
    j = pl.program_id(1)

    @pl.when(j == 0)
    def _():
        h_ref[...] = _rms(x_ref[...], g_ref[...]).astype(_BF)

    gate = jax.nn.sigmoid(_dot(h_ref[...], wg_ref[...]))
    tn = o_ref.shape[1]
    cols = pl.ds(pl.multiple_of(j * tn, tn), tn)
    o_ref[...] = x_ref[:, cols] + gate * _dot(p_ref[...].astype(_BF), wp_ref[...])


def _ple(x2, norm, p, w_gate, w_proj, layer, tm, tn):
    T, D = x2.shape
    return pl.pallas_call(
        _ple_kernel,
        grid=(T // tm, D // tn),
        in_specs=[
            pl.BlockSpec((tm, D), lambda i, j: (i, 0)),
            pl.BlockSpec((None, 1, D), lambda i, j: (layer, 0, 0)),
            pl.BlockSpec((None, tm, PLE_DIM), lambda i, j: (layer, i, 0)),
            pl.BlockSpec((None, D, tn), lambda i, j: (layer, 0, j)),
            pl.BlockSpec((None, PLE_DIM, tn), lambda i, j: (layer, 0, j)),
        ],
        out_specs=pl.BlockSpec((tm, tn), lambda i, j: (i, j)),
        out_shape=jax.ShapeDtypeStruct((T, D), _F32),
        scratch_shapes=[pltpu.VMEM((tm, D), _BF)],
        compiler_params=_params("parallel", "arbitrary"),
        name="ple",
    )(x2, norm, p, w_gate, w_proj)


def _slab_of_residue(r):
    return (r % 4) * 4 + r // 4


def _proj_kernel(x_ref, g_ref, w_ref, gain_ref, nat_ref, qkv_ref, h_ref, hp_ref, *, n_nat, n_norm):
    j = pl.program_id(1)
    tm = x_ref.shape[0]
    nsub = tm // PERM_ROWS

    @pl.when(j == 0)
    def _():
        h = _rms(x_ref[...], g_ref[...]).astype(_BF)
        h_ref[...] = h
        lam = lax.broadcasted_iota(jnp.int32, (PERM_ROWS, PERM_ROWS), 0)
        tau = lax.broadcasted_iota(jnp.int32, (PERM_ROWS, PERM_ROWS), 1)
        r = tau % ATTN_SLABS
        perm = (lam == _slab_of_residue(r) * (PERM_ROWS // ATTN_SLABS) + tau // ATTN_SLABS).astype(_BF)
        for s in range(nsub):
            rows = slice(s * PERM_ROWS, (s + 1) * PERM_ROWS)
            hp_ref[rows, :] = _dot(perm, h[rows, :]).astype(_BF)

    @pl.when(j < n_nat)
    def _():
        nat_ref[...] = _dot(h_ref[...], w_ref[...])

    def store_qkv(cols, y):
        run = PERM_ROWS // ATTN_SLABS
        for s in range(nsub):
            for slab in range(ATTN_SLABS):
                r0 = s * PERM_ROWS + slab * run
                qkv_ref[slab, s * run:(s + 1) * run, cols] = y[r0:r0 + run, :]

    is_qk = jnp.logical_and(j >= n_nat, j < n_nat + n_norm)

    @pl.when(is_qk)
    def _():
        acc = _dot(hp_ref[...], w_ref[...])
        for c in range(acc.shape[1] // HEAD_DIM):
            sl = slice(c * HEAD_DIM, (c + 1) * HEAD_DIM)
            store_qkv(sl, _rms(acc[:, sl], gain_ref[:, sl]))

    @pl.when(j >= n_nat + n_norm)
    def _():
        store_qkv(slice(None), _dot(hp_ref[...], w_ref[...]))


def _proj(x2, norm, w_in, gain, layer, tm, tn):
    T, D = x2.shape
    n_zd = (COL_GATES - COL_ZD) // tn
    n_nat = n_zd + COL_Q // tn
    n_norm = (COL_V - COL_Q) // tn
    n_all = MIX_COLS // tn
    assert tm % PERM_ROWS == 0 and ATTN_TILE % tm == 0
    per_tile = ATTN_TILE // tm

    def w_col(j):
        return jnp.where(j < n_zd, j + COL_ZD // tn, j - n_zd)

    return pl.pallas_call(
        functools.partial(_proj_kernel, n_nat=n_nat, n_norm=n_norm),
        grid=(T // tm, n_all),
        in_specs=[
            _x_spec(tm, D, T // tm),
            pl.BlockSpec((None, 1, D), lambda i, j: (layer, 0, 0)),
            pl.BlockSpec((None, None, D, tn), lambda i, j: (layer, w_col(j), 0, 0)),
            pl.BlockSpec((None, 1, tn), lambda i, j: (layer, 0, w_col(j))),
        ],
        out_specs=[
            pl.BlockSpec((tm, tn), lambda i, j: (i, jnp.minimum(j, n_nat - 1))),
            pl.BlockSpec((None, ATTN_SLABS, tm // ATTN_SLABS, tn),
                         lambda i, j: (i // per_tile, 0, i % per_tile, jnp.maximum(j - n_nat, 0))),
        ],
        out_shape=[jax.ShapeDtypeStruct((T, NAT_COLS), _F32),
                   jax.ShapeDtypeStruct((T // ATTN_TILE, ATTN_SLABS, ATTN_BLOCK, QKV_COLS), _F32)],
        scratch_shapes=[pltpu.VMEM((tm, D), _BF), pltpu.VMEM((tm, D), _BF)],
        compiler_params=_params("parallel", "arbitrary"),
        name="proj",
    )(x2, norm, w_in, gain)


def _rglru_kernel(xa_ref, halo_ref, ga_ref, cw_ref, cb_ref, wa_ref, ba_ref, wx_ref, bx_ref,
                  lam_ref, o_ref, carry_ref, sa_ref, sb_ref, xe_ref, *, tb):
    i = pl.program_id(1)

    @pl.when(i == 0)
    def _():
        carry_ref[...] = jnp.zeros_like(carry_ref)

    xe_ref[:CONV_HALO, :] = jnp.where(i == 0, 0.0, halo_ref[...])
    xe_ref[CONV_HALO:, :] = xa_ref[...]
    cw = cw_ref[...]
    xc = cb_ref[...] + cw[0:1] * xa_ref[...]
    for j in range(1, CONV_WIDTH):
        xc = xc + cw[j:j + 1] * xe_ref[pl.ds(CONV_HALO - j, tb), :]
    xcb = xc.astype(_BF)
    r_lin, i_lin = [], []
    for p in range(WIDTH // RNN_BLOCK):
        sl = slice(p * RNN_BLOCK, (p + 1) * RNN_BLOCK)
        r_lin.append(_dot(xcb[:, sl], wa_ref[p]))
        i_lin.append(_dot(xcb[:, sl], wx_ref[p]))
    r = jax.nn.sigmoid(jnp.concatenate(r_lin, axis=1) + ba_ref[...])
    ig = jax.nn.sigmoid(jnp.concatenate(i_lin, axis=1) + bx_ref[...])
    z = -lam_ref[...]
    softplus = jnp.maximum(z, 0.0) + jnp.log1p(jnp.exp(-jnp.abs(z)))
    log_a = (-RG_C * r) * softplus
    a = jnp.exp(log_a)
    y = 1.0 - a * a
    b = jnp.where(y > 0.0, y * lax.rsqrt(y), 0.0) * (ig * xc)
    ng = tb // SCAN_GROUP
    a = a.reshape(ng, SCAN_GROUP, WIDTH)
    b = b.reshape(ng, SCAN_GROUP, WIDTH)
    row = lax.broadcasted_iota(jnp.int32, (1, SCAN_GROUP, 1), 1)
    s = 1
    while s < SCAN_GROUP:
        keep = row >= s
        a_prev = jnp.where(keep, pltpu.roll(a, s, axis=1), 1.0)
        b_prev = jnp.where(keep, pltpu.roll(b, s, axis=1), 0.0)
        b = a * b_prev + b
        a = a * a_prev
        s *= 2
    sa_ref[...] = a
    sb_ref[...] = b

    def chain(g, carry):
        hg = sa_ref[g] * carry + sb_ref[g]
        sb_ref[g] = hg
        return hg[SCAN_GROUP - 1:SCAN_GROUP, :]

    carry_ref[0:1, :] = lax.fori_loop(0, ng, chain, carry_ref[0:1, :], unroll=8)
    h = sb_ref[...].reshape(tb, WIDTH)
    o_ref[...] = (h * jax.nn.gelu(ga_ref[...])).astype(_BF)


def _rglru(proj, cw, cb, wa, ba, wx, bx, lam, layer, B, S, tb):
    nb = S // tb
    hb = tb // CONV_HALO
    vec = pl.BlockSpec((None, 1, WIDTH), lambda b, i: (layer, 0, 0))
    mat = pl.BlockSpec((None, WIDTH // RNN_BLOCK, RNN_BLOCK, RNN_BLOCK), lambda b, i: (layer, 0, 0, 0))
    return pl.pallas_call(
        functools.partial(_rglru_kernel, tb=tb),
        grid=(B, nb),
        in_specs=[
            pl.BlockSpec((tb, WIDTH), lambda b, i: (b * nb + i, NAT_XA // WIDTH)),
            pl.BlockSpec((CONV_HALO, WIDTH),
                         lambda b, i: (jnp.maximum((b * nb + i) * hb - 1, 0), NAT_XA // WIDTH)),
            pl.BlockSpec((tb, WIDTH), lambda b, i: (b * nb + i, NAT_GA // WIDTH)),
            pl.BlockSpec((None, CONV_WIDTH, WIDTH), lambda b, i: (layer, 0, 0)),
            vec, mat, vec, mat, vec, vec,
        ],
        out_specs=pl.BlockSpec((tb, WIDTH), lambda b, i: (b * nb + i, 0)),
        out_shape=jax.ShapeDtypeStruct((B * S, WIDTH), _BF),
        scratch_shapes=[pltpu.VMEM((8, WIDTH), _F32),
                        pltpu.VMEM((tb // SCAN_GROUP, SCAN_GROUP, WIDTH), _F32),
                        pltpu.VMEM((tb // SCAN_GROUP, SCAN_GROUP, WIDTH), _F32),
                        pltpu.VMEM((tb + CONV_HALO, WIDTH), _F32)],
        compiler_params=_params("parallel", "arbitrary"),
        name="rglru",
    )(proj, proj, proj, cw, cb, wa, ba, wx, bx, lam)


def _pool_kernel(xb_ref, halo_ref, pw_ref, sc_ref, o_ref, *, tb):
    i = pl.program_id(1)
    x = xb_ref[...]
    halo = jnp.where(i == 0, 0.0, halo_ref[...])
    xe = jnp.concatenate([halo, x], axis=0)
    pos = i * tb + lax.broadcasted_iota(jnp.int32, (tb, 1), 0)
    for g, win in enumerate(POOL_WINDOWS):
        sl = slice(g * POOL_GROUP, (g + 1) * POOL_GROUP)
        s = xe[:, sl]
        sh = 1
        while sh < win:
            s = s + pltpu.roll(s, sh, axis=0)
            sh *= 2
        cnt = jnp.minimum(pos + 1, win).astype(_F32)
        pooled = s[POOL_HALO:] / cnt - x[:, sl]
        y = _dot(pooled.astype(_BF), pw_ref[g])
        o_ref[:, sl] = (y * sc_ref[:, sl]).astype(_BF)


def _pool(proj, pw, sc, layer, B, S, tb):
    nb = S // tb
    hb = tb // POOL_HALO
    return pl.pallas_call(
        functools.partial(_pool_kernel, tb=tb),
        grid=(B, nb),
        in_specs=[
            pl.BlockSpec((tb, WIDTH), lambda b, i: (b * nb + i, NAT_XB // WIDTH)),
            pl.BlockSpec((POOL_HALO, WIDTH),
                         lambda b, i: (jnp.maximum((b * nb + i) * hb - 1, 0), NAT_XB // WIDTH)),
            pl.BlockSpec((None, len(POOL_WINDOWS), POOL_GROUP, POOL_GROUP), lambda b, i: (layer, 0, 0, 0)),
            pl.BlockSpec((None, 1, WIDTH), lambda b, i: (layer, 0, 0)),
        ],
        out_specs=pl.BlockSpec((tb, WIDTH), lambda b, i: (b * nb + i, 0)),
        out_shape=jax.ShapeDtypeStruct((B * S, WIDTH), _BF),
        compiler_params=_params("parallel", "parallel"),
        name="pool",
    )(proj, proj, pw, sc)


def _attn_order(g):
    slab_res = np.empty(ATTN_SLABS, np.int64)
    for r in range(ATTN_SLABS):
        slab_res[_slab_of_residue(r)] = r
    if g == 0:
        qi = (slab_res[:, None] + ATTN_SLABS * np.arange(8)[None, :]).reshape(-1)
        ki = (slab_res[:, None] + ATTN_SLABS * np.arange(16)[None, :]).reshape(-1)
    elif g == 1:
        qi = (np.arange(4)[:, None] + 4 * np.arange(32)[None, :]).reshape(-1)
        ki = (np.arange(4)[:, None] + 4 * np.arange(64)[None, :]).reshape(-1)
    else:
        qi = np.arange(ATTN_BLOCK)
        ki = np.arange(2 * ATTN_BLOCK)
    return qi, ki


def _attn_bias():
    out = np.empty((2 * len(ATTN_DILATIONS), ATTN_BLOCK, 2 * ATTN_BLOCK), np.float32)
    for g in range(len(ATTN_DILATIONS)):
        qi, ki = _attn_order(g)
        band = (ki[None, :] >= qi[:, None]) & (ki[None, :] <= qi[:, None] + ATTN_BLOCK)
        out[2 * g] = np.where(band, 0.0, NEG_INF)
        out[2 * g + 1] = np.where(band & (ki[None, :] >= ATTN_BLOCK), 0.0, NEG_INF)
    return out


def _attn_kernel(q0_ref, q1_ref, q2_ref, kc_ref, kp_ref, vc_ref, vp_ref, bias_ref, o_ref,
                 ks_ref, vs_ref, o0_ref, o1_ref, o2_ref, l0_ref, l1_ref, l2_ref):
    n = pl.program_id(2)
    NJ = ATTN_BLOCK
    ks_ref[:, NJ:, :] = kc_ref[...]
    vs_ref[:, NJ:, :] = vc_ref[...]

    @pl.when(n == 0)
    def _():
        ks_ref[:, :NJ, :] = jnp.zeros((ATTN_SLABS, NJ, HEAD_DIM), _F32)
        vs_ref[:, :NJ, :] = jnp.zeros((ATTN_SLABS, NJ, HEAD_DIM), _F32)

    @pl.when(n > 0)
    def _():
        ks_ref[:, :NJ, :] = kp_ref[...]
        vs_ref[:, :NJ, :] = vp_ref[...]

    scale = HEAD_DIM ** -0.5
    exp2_scale = scale * 1.4426950408889634

    shapes = ((ATTN_SLABS, 8), (4, 32), (1, ATTN_BLOCK))
    for g, (q_ref, og_ref, lg_ref) in enumerate(((q0_ref, o0_ref, l0_ref), (q1_ref, o1_ref, l1_ref),
                                                 (q2_ref, o2_ref, l2_ref))):
        ns, nj = shapes[g]
        classes = ATTN_SLABS // ns

        def body(idx, carry, g=g, q_ref=q_ref, og_ref=og_ref, lg_ref=lg_ref, ns=ns, nj=nj, classes=classes):
            c = idx % classes
            m = idx // classes
            slabs = pl.ds(c * ns, ns)
            j0 = pl.multiple_of(m * nj, 8)
            q = q_ref[slabs, pl.ds(j0, nj), :].reshape(ATTN_BLOCK, HEAD_DIM).astype(_BF)
            kj = pl.ds(pl.multiple_of(NJ - nj + m * nj, 8), 2 * nj)
            k = ks_ref[slabs, kj, :].reshape(2 * ATTN_BLOCK, HEAD_DIM).astype(_BF)
            v = vs_ref[slabs, kj, :].reshape(2 * ATTN_BLOCK, HEAD_DIM).astype(_BF)
            first = jnp.logical_and(n == 0, m == 0).astype(jnp.int32)
            s = (lax.dot_general(q, k, (((1,), (1,)), ((), ())), preferred_element_type=_F32)
                 + bias_ref[2 * g + first])
            mx = jnp.max(s, axis=-1, keepdims=True)
            p = jnp.exp2((s - mx) * exp2_scale)
            l = jnp.sum(p, axis=-1, keepdims=True)
            o = _dot(p.astype(_BF), v) / l
            lse = mx * scale + jnp.log(l)
            og_ref[slabs, pl.ds(j0, nj), :] = o.reshape(ns, nj, HEAD_DIM)
            lg_ref[slabs, pl.ds(j0, nj), :] = jnp.broadcast_to(lse, (ATTN_BLOCK, HEAD_DIM)).reshape(ns, nj, HEAD_DIM)
            return carry

        lax.fori_loop(0, ATTN_TILE // ATTN_BLOCK, body, 0, unroll=ATTN_UNROLL)

    l0, l1, l2 = l0_ref[...], l1_ref[...], l2_ref[...]
    mx = jnp.maximum(jnp.maximum(l0, l1), l2)
    w0, w1, w2 = jnp.exp(l0 - mx), jnp.exp(l1 - mx), jnp.exp(l2 - mx)
    out = (w0 * o0_ref[...] + w1 * o1_ref[...] + w2 * o2_ref[...]) / (w0 + w1 + w2)
    for r in range(ATTN_SLABS):
        o_ref[pl.ds(r, NJ, stride=ATTN_SLABS), :] = out[_slab_of_residue(r)]


def _attn(qkv, B, S):
    nt = S // ATTN_TILE
    qb, kb, vb = QKV_Q // HEAD_DIM, QKV_K // HEAD_DIM, QKV_V // HEAD_DIM
    blk = (None, ATTN_SLABS, ATTN_BLOCK, HEAD_DIM)

    def cur(col0):
        return pl.BlockSpec(blk, lambda b, h, n: (b * nt + n, 0, 0, col0 + h))

    def prev(col0):
        return pl.BlockSpec(blk, lambda b, h, n: (b * nt + jnp.maximum(n - 1, 0), 0, 0, col0 + h))

    n_bias = 2 * len(ATTN_DILATIONS)
    big = pltpu.VMEM((ATTN_SLABS, 2 * ATTN_BLOCK, HEAD_DIM), _F32)
    tile = pltpu.VMEM((ATTN_SLABS, ATTN_BLOCK, HEAD_DIM), _F32)
    return pl.pallas_call(
        _attn_kernel,
        grid=(B, KV_HEADS, nt),
        in_specs=[cur(qb), cur(qb + KV_HEADS), cur(qb + 2 * KV_HEADS),
                  cur(kb), prev(kb), cur(vb), prev(vb),
                  pl.BlockSpec((n_bias, ATTN_BLOCK, 2 * ATTN_BLOCK), lambda b, h, n: (0, 0, 0))],
        out_specs=pl.BlockSpec((ATTN_TILE, HEAD_DIM), lambda b, h, n: (b * nt + n, h)),
        out_shape=jax.ShapeDtypeStruct((B * S, WIDTH), _F32),
        scratch_shapes=[big, big, tile, tile, tile, tile, tile, tile],
        compiler_params=_params("parallel", "parallel", "parallel"),
        name="attn",
    )(qkv, qkv, qkv, qkv, qkv, qkv, qkv, jnp.asarray(_attn_bias()))


def _sgu_kernel(z_ref, g_ref, ws_ref, b_ref, o_ref, *, tb):
    gz = jax.nn.gelu(z_ref[...])
    u = gz[:, :WIDTH]
    vv = _rms(gz[:, WIDTH:], g_ref[...]).astype(_BF)
    row = lax.broadcasted_iota(jnp.int32, (SG_CHUNK, SG_CHUNK), 0)
    col = lax.broadcasted_iota(jnp.int32, (SG_CHUNK, SG_CHUNK), 1)
    tri = row >= col
    for g in range(SG_GROUPS):
        w = jnp.where(tri, ws_ref[g], 0.0).astype(_BF)
        cs = slice(g * SG_CHUNK, (g + 1) * SG_CHUNK)
        for c in range(tb // SG_CHUNK):
            rs = slice(c * SG_CHUNK, (c + 1) * SG_CHUNK)
            mixed = _dot(w, vv[rs, cs]) + b_ref[g]
            o_ref[rs, cs] = (u[rs, cs] * mixed).astype(_BF)


def _sgu(proj, sg_norm, sg_w, sg_b, layer, T, tb):
    return pl.pallas_call(
        functools.partial(_sgu_kernel, tb=tb),
        grid=(T // tb,),
        in_specs=[
            pl.BlockSpec((tb, 2 * WIDTH), lambda i: (i, NAT_ZD // (2 * WIDTH))),
            pl.BlockSpec((None, 1, WIDTH), lambda i: (layer, 0, 0)),
            pl.BlockSpec((None, SG_GROUPS, SG_CHUNK, SG_CHUNK), lambda i: (layer, 0, 0, 0)),
            pl.BlockSpec((None, SG_GROUPS, SG_CHUNK, 1), lambda i: (layer, 0, 0, 0)),
        ],
        out_specs=pl.BlockSpec((tb, WIDTH), lambda i: (i, 0)),
        out_shape=jax.ShapeDtypeStruct((T, WIDTH), _BF),
        compiler_params=_params("parallel"),
        name="sgu",
    )(proj, sg_norm, sg_w, sg_b)


def _merge_kernel(x_ref, g_ref, ya_ref, yb_ref, yc_ref, yd_ref, wg0_ref, wg1_ref, wg2_ref, wg3_ref,
                  bg_ref, wb_ref, wo_ref, o_ref, h_ref):
    j = pl.program_id(1)

    @pl.when(j == 0)
    def _():
        h_ref[...] = _rms(x_ref[...], g_ref[...]).astype(_BF)

    def step(first):
        h = h_ref[...]
        merged = None
        for b, (y_ref, wg_ref) in enumerate(((ya_ref, wg0_ref), (yb_ref, wg1_ref),
                                             (yc_ref, wg2_ref), (yd_ref, wg3_ref))):
            gate = jax.nn.sigmoid(_dot(h, wg_ref[...]) + bg_ref[b:b + 1, :])
            term = gate * _dot(y_ref[...].astype(_BF), wb_ref[b])
            merged = term if merged is None else merged + term
        base = x_ref[...] if first else o_ref[...]
        o_ref[...] = base + _dot(merged.astype(_BF), wo_ref[...])

    pl.when(j == 0)(lambda: step(True))
    pl.when(j > 0)(lambda: step(False))


def _merge(x2, norm, ys, w_gates, b_gate, w_branch, w_out, layer, tm, tn):
    T, D = x2.shape
    y_spec = pl.BlockSpec((tm, WIDTH), lambda i, j: (i, 0))

    n_cols = D // tn

    def gate_spec(b):
        return pl.BlockSpec((None, None, D, tn), lambda i, j: (layer, b * n_cols + j, 0, 0))

    return pl.pallas_call(
        _merge_kernel,
        grid=(T // tm, D // tn),
        in_specs=[
            _x_spec(tm, D, T // tm),
            pl.BlockSpec((None, 1, D), lambda i, j: (layer, 0, 0)),
            y_spec, y_spec, y_spec, y_spec,
            gate_spec(0), gate_spec(1), gate_spec(2), gate_spec(3),
            pl.BlockSpec((None, N_BRANCH, tn), lambda i, j: (layer, 0, j)),
            pl.BlockSpec((None, N_BRANCH, None, WIDTH, tn), lambda i, j: (layer, 0, j, 0, 0)),
            pl.BlockSpec((None, tn, D), lambda i, j: (layer, j, 0)),
        ],
        out_specs=pl.BlockSpec((tm, D), lambda i, j: (i, 0)),
        out_shape=jax.ShapeDtypeStruct((T, D), _F32),
        scratch_shapes=[pltpu.VMEM((tm, D), _BF)],
        compiler_params=_params("parallel", "arbitrary"),
        name="merge",
    )(x2, norm, *ys, w_gates, w_gates, w_gates, w_gates, b_gate, w_branch, w_out)


def _block_diag(w):
    L = w.shape[0]
    per = RNN_BLOCK // RNN_HEAD_DIM
    w = w.reshape(L, RNN_HEADS // per, per, RNN_HEAD_DIM, RNN_HEAD_DIM)
    eye = jnp.eye(per, dtype=w.dtype)
    bd = jnp.einsum('lphij,hk->lphikj', w, eye)
    return bd.reshape(L, RNN_HEADS // per, RNN_BLOCK, RNN_BLOCK).astype(_BF)


def kernel(x, p, ffn1_norm, ffn1_w1, ffn1_w3, ffn1_w2, mix_norm, w_in, b_gate, conv_w, conv_b, rg_wa, rg_ba, rg_wx, rg_bx, rg_lambda, pool_w, pool_scale, q_gain, k_gain, sg_norm, sg_w, sg_b, w_branch, w_out, ffn2_norm, ffn2_w1, ffn2_w3, ffn2_w2, ple_norm, ple_gate_w, ple_proj):
    B, S, D = x.shape
    L = w_in.shape[0]
    T = B * S
    assert D == D_MODEL and S % ATTN_TILE == 0
    tm = 512

    def vec(a):
        return a.reshape(L, 1, a.shape[-1])

    f1 = (ffn1_w1.astype(_BF), ffn1_w3.astype(_BF), ffn1_w2.astype(_BF))
    f2 = (ffn2_w1.astype(_BF), ffn2_w3.astype(_BF), ffn2_w2.astype(_BF))
    w_mix_b = w_in[:, :, :COL_GATES].astype(_BF).reshape(L, D, -1, PROJ_TILE).transpose(0, 2, 1, 3)
    w_gates_b = w_in[:, :, COL_GATES:].astype(_BF).reshape(L, D, -1, MERGE_TILE).transpose(0, 2, 1, 3)
    w_branch_b = w_branch.astype(_BF).reshape(L, N_BRANCH, WIDTH, -1, MERGE_TILE).transpose(0, 1, 3, 2, 4)
    w_out_b = w_out.astype(_BF)
    ple_gate_b = ple_gate_w.astype(_BF)
    ple_proj_b = ple_proj.astype(_BF)
    pool_w_b = pool_w.astype(_BF)
    wa_bd, wx_bd = _block_diag(rg_wa), _block_diag(rg_wx)
    ones = jnp.ones((L, COL_Q), _F32)
    qk_gain = jnp.concatenate(
        [ones, jnp.tile(q_gain, (1, (COL_K - COL_Q) // HEAD_DIM)),
         jnp.tile(k_gain, (1, (COL_V - COL_K) // HEAD_DIM)),
         jnp.ones((L, MIX_COLS - COL_V), _F32)], axis=1).reshape(L, 1, MIX_COLS)
    p2 = p.reshape(L, T, PLE_DIM)
    sg_b4 = sg_b.reshape(L, SG_GROUPS, SG_CHUNK, 1)

    x2 = x.reshape(T, D)
    for i in range(L):
        x2 = _ffn(x2, vec(ffn1_norm), *f1, i, 1024)
        proj, qkv = _proj(x2, vec(mix_norm), w_mix_b, qk_gain, i, 1024, PROJ_TILE)
        ya = _rglru(proj, conv_w, vec(conv_b), wa_bd, vec(rg_ba), wx_bd, vec(rg_bx), vec(rg_lambda),
                    i, B, S, 256)
        yb = _pool(proj, pool_w_b, vec(pool_scale), i, B, S, 512)
        yc = _attn(qkv, B, S)
        yd = _sgu(proj, vec(sg_norm), sg_w, sg_b4, i, T, 512)
        x2 = _merge(x2, vec(mix_norm), (ya, yb, yc, yd), w_gates_b, b_gate, w_branch_b, w_out_b, i, tm, MERGE_TILE)
        x2 = _ffn(x2, vec(ffn2_norm), *f2, i, 1024)
        x2 = _ple(x2, vec(ple_norm), p2, ple_gate_b, ple_proj_b, i, 1024, PLE_TILE)
    return x2.reshape(B, S, D)
```

```python
import functools

import numpy as np

import jax
import jax.numpy as jnp
from jax import lax
from jax.experimental import pallas as pl
from jax.experimental.pallas import tpu as pltpu

EPS = 1e-6
NEG_INF = -1e30
D_MODEL = 2048
D_FF = 5504
FF_TILE = 256
FF_CHUNK = 512
PLE_TILE = 1024
PLE_DIM = 256
WIDTH = 1024
RNN_HEADS = 16
RNN_HEAD_DIM = 64
RNN_BLOCK = 256
RG_C = 8.0
CONV_WIDTH = 4
POOL_WINDOWS = (2, 4, 8, 16)
POOL_GROUP = 256
POOL_HALO = 16
CONV_HALO = 8
SCAN_GROUP = 8
HEAD_DIM = 128
KV_HEADS = 8
ATTN_DILATIONS = (1, 4, 16)
ATTN_BLOCK = 128
ATTN_TILE = ATTN_BLOCK * max(ATTN_DILATIONS)
ATTN_UNROLL = 16
SG_CHUNK = 128
SG_GROUPS = 8
N_BRANCH = 4
COL_XA, COL_GA, COL_XB, COL_Q, COL_K, COL_V, COL_ZD, COL_GATES = (
    0, 1024, 2048, 3072, 6144, 7168, 8192, 10240)
MIX_COLS = COL_GATES
NAT_ZD, NAT_XA, NAT_GA, NAT_XB, NAT_COLS = 0, 2048, 3072, 4096, 5120
QKV_Q, QKV_K, QKV_V, QKV_COLS = 0, 3072, 4096, 5120
ATTN_SLABS = 16
PERM_ROWS = 256
VMEM_LIMIT = 56 * 1024 * 1024

_BF = jnp.bfloat16
_F32 = jnp.float32


def _params(*sem):
    return pltpu.CompilerParams(dimension_semantics=sem, vmem_limit_bytes=VMEM_LIMIT)


def _rms(x, g):
    return x * lax.rsqrt(jnp.mean(x * x, axis=-1, keepdims=True) + EPS) * g


def _dot(a, b):
    return jnp.dot(a, b, preferred_element_type=_F32)


def _x_spec(tm, d, n_tiles):
    last = n_tiles - 1
    return pl.BlockSpec((tm, d), lambda i, j: (jnp.where(j == 0, i, jnp.minimum(i + 1, last)), 0))


def _ffn_kernel(x_ref, g_ref, w1_ref, w3_ref, w2_ref, o_ref, h_ref, *, nf):
    j = pl.program_id(1)
    tf = w1_ref.shape[1]

    @pl.when(j == 0)
    def _():
        h_ref[...] = _rms(x_ref[...], g_ref[...]).astype(_BF)

    def ffn_step(valid, first):
        h = h_ref[...]
        a = _dot(h, w1_ref[:, :valid].astype(_BF))
        b = _dot(h, w3_ref[:, :valid].astype(_BF))
        act = (0.5 * (a * jax.nn.sigmoid(a)) * b).astype(_BF)
        for c in range(o_ref.shape[1] // FF_CHUNK):
            sl = slice(c * FF_CHUNK, (c + 1) * FF_CHUNK)
            base = x_ref[:, sl] if first else o_ref[:, sl]
            o_ref[:, sl] = base + _dot(act, w2_ref[:valid, sl].astype(_BF))

    last_valid = D_FF - (nf - 1) * tf
    pl.when(j == 0)(lambda: ffn_step(tf, True))
    pl.when(jnp.logical_and(j > 0, j < nf - 1))(lambda: ffn_step(tf, False))
    pl.when(j == nf - 1)(lambda: ffn_step(last_valid, False))


def _ffn(x2, norm, w1, w3, w2, layer, tm):
    T, D = x2.shape
    tf = FF_TILE
    nf = -(-D_FF // tf)
    return pl.pallas_call(
        functools.partial(_ffn_kernel, nf=nf),
        grid=(T // tm, nf),
        in_specs=[
            _x_spec(tm, D, T // tm),
            pl.BlockSpec((None, 1, D), lambda i, j: (layer, 0, 0)),
            pl.BlockSpec((None, D, tf), lambda i, j: (layer, 0, j)),
            pl.BlockSpec((None, D, tf), lambda i, j: (layer, 0, j)),
            pl.BlockSpec((None, tf, D), lambda i, j: (layer, j, 0)),
        ],
        out_specs=pl.BlockSpec((tm, D), lambda i, j: (i, 0)),
        out_shape=jax.ShapeDtypeStruct((T, D), _F32),
        scratch_shapes=[pltpu.VMEM((tm, D), _BF)],
        compiler_params=_params("parallel", "arbitrary"),
        name="ffn",
    )(x2, norm, w1, w3, w2)


def _ple_kernel(x_ref, g_ref, p_ref, wg_ref, wp_ref, o_ref, h_ref):
    j = pl.program_id(1)

    @pl.when(j == 0)
    def _():
        h_ref[...] = _rms(x_ref[...], g_ref[...]).astype(_BF)

    gate = jax.nn.sigmoid(_dot(h_ref[...], wg_ref[...]))
    tn = o_ref.shape[1]
    cols = pl.ds(pl.multiple_of(j * tn, tn), tn)
    o_ref[...] = x_ref[:, cols] + gate * _dot(p_ref[...].astype(_BF), wp_ref[...])


def _ple(x2, norm, p, w_gate, w_proj, layer, tm, tn):
    T, D = x2.shape
    return pl.pallas_call(
        _ple_kernel,
        grid=(T // tm, D // tn),
        in_specs=[
            pl.BlockSpec((tm, D), lambda i, j: (i, 0)),
            pl.BlockSpec((None, 1, D), lambda i, j: (layer, 0, 0)),
            pl.BlockSpec((None, tm, PLE_DIM), lambda i, j: (layer, i, 0)),
            pl.BlockSpec((None, D, tn), lambda i, j: (layer, 0, j)),
            pl.BlockSpec((None, PLE_DIM, tn), lambda i, j: (layer, 0, j)),
        ],
        out_specs=pl.BlockSpec((tm, tn), lambda i, j: (i, j)),
        out_shape=jax.ShapeDtypeStruct((T, D), _F32),
        scratch_shapes=[pltpu.VMEM((tm, D), _BF)],
        compiler_params=_params("parallel", "arbitrary"),
        name="ple",
    )(x2, norm, p, w_gate, w_proj)


def _slab_of_residue(r):
    return (r % 4) * 4 + r // 4


def _proj_kernel(x_ref, g_ref, w_ref, gain_ref, nat_ref, qkv_ref, h_ref, hp_ref, *, n_nat, n_norm):
    j = pl.program_id(1)
    tm = x_ref.shape[0]
    nsub = tm // PERM_ROWS

    @pl.when(j == 0)
    def _():
        h = _rms(x_ref[...], g_ref[...]).astype(_BF)
        h_ref[...] = h
        lam = lax.broadcasted_iota(jnp.int32, (PERM_ROWS, PERM_ROWS), 0)
        tau = lax.broadcasted_iota(jnp.int32, (PERM_ROWS, PERM_ROWS), 1)
        r = tau % ATTN_SLABS
        perm = (lam == _slab_of_residue(r) * (PERM_ROWS // ATTN_SLABS) + tau // ATTN_SLABS).astype(_BF)
        for s in range(nsub):
            rows = slice(s * PERM_ROWS, (s + 1) * PERM_ROWS)
            hp_ref[rows, :] = _dot(perm, h[rows, :]).astype(_BF)

    @pl.when(j < n_nat)
    def _():
        nat_ref[...] = _dot(h_ref[...], w_ref[...])

    def store_qkv(cols, y):
        run = PERM_ROWS // ATTN_SLABS
        for s in range(nsub):
            for slab in range(ATTN_SLABS):
                r0 = s * PERM_ROWS + slab * run
                qkv_ref[slab, s * run:(s + 1) * run, cols] = y[r0:r0 + run, :]

    is_qk = jnp.logical_and(j >= n_nat, j < n_nat + n_norm)

    @pl.when(is_qk)
    def _():
        acc = _dot(hp_ref[...], w_ref[...])
        for c in range(acc.shape[1] // HEAD_DIM):
            sl = slice(c * HEAD_DIM, (c + 1) * HEAD_DIM)
            store_qkv(sl, _rms(acc[:, sl], gain_ref[:, sl]))

    @pl.when(j >= n_nat + n_norm)
    def _():
        store_qkv(slice(None), _dot(hp_ref[...], w_ref[...]))


def _proj(x2, norm, w_in, gain, layer, tm, tn):
    T, D = x2.shape
    n_zd = (COL_GATES - COL_ZD) // tn
    n_nat = n_zd + COL_Q // tn
    n_norm = (COL_V - COL_Q) // tn
    n_all = MIX_COLS // tn
    assert tm % PERM_ROWS == 0 and ATTN_TILE % tm == 0
    per_tile = ATTN_TILE // tm

    def w_col(j):
        return jnp.where(j < n_zd, j + COL_ZD // tn, j - n_zd)

    return pl.pallas_call(
        functools.partial(_proj_kernel, n_nat=n_nat, n_norm=n_norm),
        grid=(T // tm, n_all),
        in_specs=[
            _x_spec(tm, D, T // tm),
            pl.BlockSpec((None, 1, D), lambda i, j: (layer, 0, 0)),
            pl.BlockSpec((None, D, tn), lambda i, j: (layer, 0, w_col(j))),
            pl.BlockSpec((None, 1, tn), lambda i, j: (layer, 0, w_col(j))),
        ],
        out_specs=[
            pl.BlockSpec((tm, tn), lambda i, j: (i, jnp.minimum(j, n_nat - 1))),
            pl.BlockSpec((None, ATTN_SLABS, tm // ATTN_SLABS, tn),
                         lambda i, j: (i // per_tile, 0, i % per_tile, jnp.maximum(j - n_nat, 0))),
        ],
        out_shape=[jax.ShapeDtypeStruct((T, NAT_COLS), _F32),
                   jax.ShapeDtypeStruct((T // ATTN_TILE, ATTN_SLABS, ATTN_BLOCK, QKV_COLS), _F32)],
        scratch_shapes=[pltpu.VMEM((tm, D), _BF), pltpu.VMEM((tm, D), _BF)],
        compiler_params=_params("parallel", "arbitrary"),
        name="proj",
    )(x2, norm, w_in, gain)


def _rglru_kernel(xa_ref, halo_ref, ga_ref, cw_ref, cb_ref, wa_ref, ba_ref, wx_ref, bx_ref,
                  lam_ref, o_ref, carry_ref, sa_ref, sb_ref, xe_ref, *, tb):
    i = pl.program_id(1)

    @pl.when(i == 0)
    def _():
        carry_ref[...] = jnp.zeros_like(carry_ref)

    xe_ref[:CONV_HALO, :] = jnp.where(i == 0, 0.0, halo_ref[...])
    xe_ref[CONV_HALO:, :] = xa_ref[...]
    cw = cw_ref[...]
    xc = cb_ref[...] + cw[0:1] * xa_ref[...]
    for j in range(1, CONV_WIDTH):
        xc = xc + cw[j:j + 1] * xe_ref[pl.ds(CONV_HALO - j, tb), :]
    xcb = xc.astype(_BF)
    r_lin, i_lin = [], []
    for p in range(WIDTH // RNN_BLOCK):
        sl = slice(p * RNN_BLOCK, (p + 1) * RNN_BLOCK)
        r_lin.append(_dot(xcb[:, sl], wa_ref[p]))
        i_lin.append(_dot(xcb[:, sl], wx_ref[p]))
    r = jax.nn.sigmoid(jnp.concatenate(r_lin, axis=1) + ba_ref[...])
    ig = jax.nn.sigmoid(jnp.concatenate(i_lin, axis=1) + bx_ref[...])
    z = -lam_ref[...]
    softplus = jnp.maximum(z, 0.0) + jnp.log1p(jnp.exp(-jnp.abs(z)))
    log_a = (-RG_C * r) * softplus
    a = jnp.exp(log_a)
    y = 1.0 - a * a
    b = jnp.where(y > 0.0, y * lax.rsqrt(y), 0.0) * (ig * xc)
    ng = tb // SCAN_GROUP
    a = a.reshape(ng, SCAN_GROUP, WIDTH)
    b = b.reshape(ng, SCAN_GROUP, WIDTH)
    row = lax.broadcasted_iota(jnp.int32, (1, SCAN_GROUP, 1), 1)
    s = 1
    while s < SCAN_GROUP:
        keep = row >= s
        a_prev = jnp.where(keep, pltpu.roll(a, s, axis=1), 1.0)
        b_prev = jnp.where(keep, pltpu.roll(b, s, axis=1), 0.0)
        b = a * b_prev + b
        a = a * a_prev
        s *= 2
    sa_ref[...] = a
    sb_ref[...] = b

    def chain(g, carry):
        hg = sa_ref[g] * carry + sb_ref[g]
        sb_ref[g] = hg
        return hg[SCAN_GROUP - 1:SCAN_GROUP, :]

    carry_ref[0:1, :] = lax.fori_loop(0, ng, chain, carry_ref[0:1, :], unroll=8)
    h = sb_ref[...].reshape(tb, WIDTH)
    o_ref[...] = (h * jax.nn.gelu(ga_ref[...])).astype(_BF)


def _rglru(proj, cw, cb, wa, ba, wx, bx, lam, layer, B, S, tb):
    nb = S // tb
    hb = tb // CONV_HALO
    vec = pl.BlockSpec((None, 1, WIDTH), lambda b, i: (layer, 0, 0))
    mat = pl.BlockSpec((None, WIDTH // RNN_BLOCK, RNN_BLOCK, RNN_BLOCK), lambda b, i: (layer, 0, 0, 0))
    return pl.pallas_call(
        functools.partial(_rglru_kernel, tb=tb),
        grid=(B, nb),
        in_specs=[
            pl.BlockSpec((tb, WIDTH), lambda b, i: (b * nb + i, NAT_XA // WIDTH)),
            pl.BlockSpec((CONV_HALO, WIDTH),
                         lambda b, i: (jnp.maximum((b * nb + i) * hb - 1, 0), NAT_XA // WIDTH)),
            pl.BlockSpec((tb, WIDTH), lambda b, i: (b * nb + i, NAT_GA // WIDTH)),
            pl.BlockSpec((None, CONV_WIDTH, WIDTH), lambda b, i: (layer, 0, 0)),
            vec, mat, vec, mat, vec, vec,
        ],
        out_specs=pl.BlockSpec((tb, WIDTH), lambda b, i: (b * nb + i, 0)),
        out_shape=jax.ShapeDtypeStruct((B * S, WIDTH), _BF),
        scratch_shapes=[pltpu.VMEM((8, WIDTH), _F32),
                        pltpu.VMEM((tb // SCAN_GROUP, SCAN_GROUP, WIDTH), _F32),
                        pltpu.VMEM((tb // SCAN_GROUP, SCAN_GROUP, WIDTH), _F32),
                        pltpu.VMEM((tb + CONV_HALO, WIDTH), _F32)],
        compiler_params=_params("parallel", "arbitrary"),
        name="rglru",
    )(proj, proj, proj, cw, cb, wa, ba, wx, bx, lam)


def _pool_kernel(xb_ref, halo_ref, pw_ref, sc_ref, o_ref, *, tb):
    i = pl.program_id(1)
    x = xb_ref[...]
    halo = jnp.where(i == 0, 0.0, halo_ref[...])
    xe = jnp.concatenate([halo, x], axis=0)
    pos = i * tb + lax.broadcasted_iota(jnp.int32, (tb, 1), 0)
    for g, win in enumerate(POOL_WINDOWS):
        sl = slice(g * POOL_GROUP, (g + 1) * POOL_GROUP)
        s = xe[:, sl]
        sh = 1
        while sh < win:
            s = s + pltpu.roll(s, sh, axis=0)
            sh *= 2
        cnt = jnp.minimum(pos + 1, win).astype(_F32)
        pooled = s[POOL_HALO:] / cnt - x[:, sl]
        y = _dot(pooled.astype(_BF), pw_ref[g])
        o_ref[:, sl] = (y * sc_ref[:, sl]).astype(_BF)


def _pool(proj, pw, sc, layer, B, S, tb):
    nb = S // tb
    hb = tb // POOL_HALO
    return pl.pallas_call(
        functools.partial(_pool_kernel, tb=tb),
        grid=(B, nb),
        in_specs=[
            pl.BlockSpec((tb, WIDTH), lambda b, i: (b * nb + i, NAT_XB // WIDTH)),
            pl.BlockSpec((POOL_HALO, WIDTH),
                         lambda b, i: (jnp.maximum((b * nb + i) * hb - 1, 0), NAT_XB // WIDTH)),
            pl.BlockSpec((None, len(POOL_WINDOWS), POOL_GROUP, POOL_GROUP), lambda b, i: (layer, 0, 0, 0)),
            pl.BlockSpec((None, 1, WIDTH), lambda b, i: (layer, 0, 0)),
        ],
        out_specs=pl.BlockSpec((tb, WIDTH), lambda b, i: (b * nb + i, 0)),
        out_shape=jax.ShapeDtypeStruct((B * S, WIDTH), _BF),
        compiler_params=_params("parallel", "parallel"),
        name="pool",
    )(proj, proj, pw, sc)


def _attn_order(g):
    slab_res = np.empty(ATTN_SLABS, np.int64)
    for r in range(ATTN_SLABS):
        slab_res[_slab_of_residue(r)] = r
    if g == 0:
        qi = (slab_res[:, None] + ATTN_SLABS * np.arange(8)[None, :]).reshape(-1)
        ki = (slab_res[:, None] + ATTN_SLABS * np.arange(16)[None, :]).reshape(-1)
    elif g == 1:
        qi = (np.arange(4)[:, None] + 4 * np.arange(32)[None, :]).reshape(-1)
        ki = (np.arange(4)[:, None] + 4 * np.arange(64)[None, :]).reshape(-1)
    else:
        qi = np.arange(ATTN_BLOCK)
        ki = np.arange(2 * ATTN_BLOCK)
    return qi, ki


def _attn_bias():
    out = np.empty((2 * len(ATTN_DILATIONS), ATTN_BLOCK, 2 * ATTN_BLOCK), np.float32)
    for g in range(len(ATTN_DILATIONS)):
        qi, ki = _attn_order(g)
        band = (ki[None, :] >= qi[:, None]) & (ki[None, :] <= qi[:, None] + ATTN_BLOCK)
        out[2 * g] = np.where(band, 0.0, NEG_INF)
        out[2 * g + 1] = np.where(band & (ki[None, :] >= ATTN_BLOCK), 0.0, NEG_INF)
    return out


def _attn_kernel(q0_ref, q1_ref, q2_ref, kc_ref, kp_ref, vc_ref, vp_ref, bias_ref, o_ref,
                 ks_ref, vs_ref, o0_ref, o1_ref, o2_ref, l0_ref, l1_ref, l2_ref):
    n = pl.program_id(2)
    NJ = ATTN_BLOCK
    ks_ref[:, NJ:, :] = kc_ref[...]
    vs_ref[:, NJ:, :] = vc_ref[...]

    @pl.when(n == 0)
    def _():
        ks_ref[:, :NJ, :] = jnp.zeros((ATTN_SLABS, NJ, HEAD_DIM), _F32)
        vs_ref[:, :NJ, :] = jnp.zeros((ATTN_SLABS, NJ, HEAD_DIM), _F32)

    @pl.when(n > 0)
    def _():
        ks_ref[:, :NJ, :] = kp_ref[...]
        vs_ref[:, :NJ, :] = vp_ref[...]

    scale = HEAD_DIM ** -0.5
    exp2_scale = scale * 1.4426950408889634

    shapes = ((ATTN_SLABS, 8), (4, 32), (1, ATTN_BLOCK))
    for g, (q_ref, og_ref, lg_ref) in enumerate(((q0_ref, o0_ref, l0_ref), (q1_ref, o1_ref, l1_ref),
                                                 (q2_ref, o2_ref, l2_ref))):
        ns, nj = shapes[g]
        classes = ATTN_SLABS // ns

        def body(idx, carry, g=g, q_ref=q_ref, og_ref=og_ref, lg_ref=lg_ref, ns=ns, nj=nj, classes=classes):
            c = idx % classes
            m = idx // classes
            slabs = pl.ds(c * ns, ns)
            j0 = pl.multiple_of(m * nj, 8)
            q = q_ref[slabs, pl.ds(j0, nj), :].reshape(ATTN_BLOCK, HEAD_DIM).astype(_BF)
            kj = pl.ds(pl.multiple_of(NJ - nj + m * nj, 8), 2 * nj)
            k = ks_ref[slabs, kj, :].reshape(2 * ATTN_BLOCK, HEAD_DIM).astype(_BF)
            v = vs_ref[slabs, kj, :].reshape(2 * ATTN_BLOCK, HEAD_DIM).astype(_BF)
            first = jnp.logical_and(n == 0, m == 0).astype(jnp.int32)
            s = (lax.dot_general(q, k, (((1,), (1,)), ((), ())), preferred_element_type=_F32)
                 + bias_ref[2 * g + first])
            mx = jnp.max(s, axis=-1, keepdims=True)
            p = jnp.exp2((s - mx) * exp2_scale)
            l = jnp.sum(p, axis=-1, keepdims=True)
            o = _dot(p.astype(_BF), v) / l
            lse = mx * scale + jnp.log(l)
            og_ref[slabs, pl.ds(j0, nj), :] = o.reshape(ns, nj, HEAD_DIM)
            lg_ref[slabs, pl.ds(j0, nj), :] = jnp.broadcast_to(lse, (ATTN_BLOCK, HEAD_DIM)).reshape(ns, nj, HEAD_DIM)
            return carry

        lax.fori_loop(0, ATTN_TILE // ATTN_BLOCK, body, 0, unroll=ATTN_UNROLL)

    l0, l1, l2 = l0_ref[...], l1_ref[...], l2_ref[...]
    mx = jnp.maximum(jnp.maximum(l0, l1), l2)
    w0, w1, w2 = jnp.exp(l0 - mx), jnp.exp(l1 - mx), jnp.exp(l2 - mx)
    out = (w0 * o0_ref[...] + w1 * o1_ref[...] + w2 * o2_ref[...]) / (w0 + w1 + w2)
    for r in range(ATTN_SLABS):
        o_ref[pl.ds(r, NJ, stride=ATTN_SLABS), :] = out[_slab_of_residue(r)]


def _attn(qkv, B, S):
    nt = S // ATTN_TILE
    qb, kb, vb = QKV_Q // HEAD_DIM, QKV_K // HEAD_DIM, QKV_V // HEAD_DIM
    blk = (None, ATTN_SLABS, ATTN_BLOCK, HEAD_DIM)

    def cur(col0):
        return pl.BlockSpec(blk, lambda b, h, n: (b * nt + n, 0, 0, col0 + h))

    def prev(col0):
        return pl.BlockSpec(blk, lambda b, h, n: (b * nt + jnp.maximum(n - 1, 0), 0, 0, col0 + h))

    n_bias = 2 * len(ATTN_DILATIONS)
    big = pltpu.VMEM((ATTN_SLABS, 2 * ATTN_BLOCK, HEAD_DIM), _F32)
    tile = pltpu.VMEM((ATTN_SLABS, ATTN_BLOCK, HEAD_DIM), _F32)
    return pl.pallas_call(
        _attn_kernel,
        grid=(B, KV_HEADS, nt),
        in_specs=[cur(qb), cur(qb + KV_HEADS), cur(qb + 2 * KV_HEADS),
                  cur(kb), prev(kb), cur(vb), prev(vb),
                  pl.BlockSpec((n_bias, ATTN_BLOCK, 2 * ATTN_BLOCK), lambda b, h, n: (0, 0, 0))],
        out_specs=pl.BlockSpec((ATTN_TILE, HEAD_DIM), lambda b, h, n: (b * nt + n, h)),
        out_shape=jax.ShapeDtypeStruct((B * S, WIDTH), _F32),
        scratch_shapes=[big, big, tile, tile, tile, tile, tile, tile],
        compiler_params=_params("parallel", "parallel", "parallel"),
        name="attn",
    )(qkv, qkv, qkv, qkv, qkv, qkv, qkv, jnp.asarray(_attn_bias()))


def _sgu_kernel(z_ref, g_ref, ws_ref, b_ref, o_ref, *, tb):
    gz = jax.nn.gelu(z_ref[...])
    u = gz[:, :WIDTH]
    vv = _rms(gz[:, WIDTH:], g_ref[...]).astype(_BF)
    row = lax.broadcasted_iota(jnp.int32, (SG_CHUNK, SG_CHUNK), 0)
    col = lax.broadcasted_iota(jnp.int32, (SG_CHUNK, SG_CHUNK), 1)
    tri = row >= col
    for g in range(SG_GROUPS):
        w = jnp.where(tri, ws_ref[g], 0.0).astype(_BF)
        cs = slice(g * SG_CHUNK, (g + 1) * SG_CHUNK)
        for c in range(tb // SG_CHUNK):
            rs = slice(c * SG_CHUNK, (c + 1) * SG_CHUNK)
            mixed = _dot(w, vv[rs, cs]) + b_ref[g]
            o_ref[rs, cs] = (u[rs, cs] * mixed).astype(_BF)


def _sgu(proj, sg_norm, sg_w, sg_b, layer, T, tb):
    return pl.pallas_call(
        functools.partial(_sgu_kernel, tb=tb),
        grid=(T // tb,),
        in_specs=[
            pl.BlockSpec((tb, 2 * WIDTH), lambda i: (i, NAT_ZD // (2 * WIDTH))),
            pl.BlockSpec((None, 1, WIDTH), lambda i: (layer, 0, 0)),
            pl.BlockSpec((None, SG_GROUPS, SG_CHUNK, SG_CHUNK), lambda i: (layer, 0, 0, 0)),
            pl.BlockSpec((None, SG_GROUPS, SG_CHUNK, 1), lambda i: (layer, 0, 0, 0)),
        ],
        out_specs=pl.BlockSpec((tb, WIDTH), lambda i: (i, 0)),
        out_shape=jax.ShapeDtypeStruct((T, WIDTH), _BF),
        compiler_params=_params("parallel"),
        name="sgu",
    )(proj, sg_norm, sg_w, sg_b)


def _merge_kernel(x_ref, g_ref, ya_ref, yb_ref, yc_ref, yd_ref, wg0_ref, wg1_ref, wg2_ref, wg3_ref,
                  bg_ref, wb_ref, wo_ref, o_ref, h_ref):
    j = pl.program_id(1)

    @pl.when(j == 0)
    def _():
        h_ref[...] = _rms(x_ref[...], g_ref[...]).astype(_BF)

    def step(first):
        h = h_ref[...]
        merged = None
        for b, (y_ref, wg_ref) in enumerate(((ya_ref, wg0_ref), (yb_ref, wg1_ref),
                                             (yc_ref, wg2_ref), (yd_ref, wg3_ref))):
            gate = jax.nn.sigmoid(_dot(h, wg_ref[...]) + bg_ref[b:b + 1, :])
            term = gate * _dot(y_ref[...].astype(_BF), wb_ref[b])
            merged = term if merged is None else merged + term
        base = x_ref[...] if first else o_ref[...]
        o_ref[...] = base + _dot(merged.astype(_BF), wo_ref[...])

    pl.when(j == 0)(lambda: step(True))
    pl.when(j > 0)(lambda: step(False))


def _merge(x2, norm, ys, w_in, b_gate, w_branch, w_out, layer, tm, tn):
    T, D = x2.shape
    y_spec = pl.BlockSpec((tm, WIDTH), lambda i, j: (i, 0))

    def gate_spec(b):
        col0 = (COL_GATES + b * D) // tn
        return pl.BlockSpec((None, D, tn), lambda i, j: (layer, 0, col0 + j))

    return pl.pallas_call(
        _merge_kernel,
        grid=(T // tm, D // tn),
        in_specs=[
            _x_spec(tm, D, T // tm),
            pl.BlockSpec((None, 1, D), lambda i, j: (layer, 0, 0)),
            y_spec, y_spec, y_spec, y_spec,
            gate_spec(0), gate_spec(1), gate_spec(2), gate_spec(3),
            pl.BlockSpec((None, N_BRANCH, tn), lambda i, j: (layer, 0, j)),
            pl.BlockSpec((None, N_BRANCH, WIDTH, tn), lambda i, j: (layer, 0, 0, j)),
            pl.BlockSpec((None, tn, D), lambda i, j: (layer, j, 0)),
        ],
        out_specs=pl.BlockSpec((tm, D), lambda i, j: (i, 0)),
        out_shape=jax.ShapeDtypeStruct((T, D), _F32),
        scratch_shapes=[pltpu.VMEM((tm, D), _BF)],
        compiler_params=_params("parallel", "arbitrary"),
        name="merge",
    )(x2, norm, *ys, w_in, w_in, w_in, w_in, b_gate, w_branch, w_out)


def _block_diag(w):
    L = w.shape[0]
    per = RNN_BLOCK // RNN_HEAD_DIM
    w = w.reshape(L, RNN_HEADS // per, per, RNN_HEAD_DIM, RNN_HEAD_DIM)
    eye = jnp.eye(per, dtype=w.dtype)
    bd = jnp.einsum('lphij,hk->lphikj', w, eye)
    return bd.reshape(L, RNN_HEADS // per, RNN_BLOCK, RNN_BLOCK).astype(_BF)


def kernel(x, p, ffn1_norm, ffn1_w1, ffn1_w3, ffn1_w2, mix_norm, w_in, b_gate, conv_w, conv_b, rg_wa, rg_ba, rg_wx, rg_bx, rg_lambda, pool_w, pool_scale, q_gain, k_gain, sg_norm, sg_w, sg_b, w_branch, w_out, ffn2_norm, ffn2_w1, ffn2_w3, ffn2_w2, ple_norm, ple_gate_w, ple_proj):
    B, S, D = x.shape
    L = w_in.shape[0]
    T = B * S
    assert D == D_MODEL and S % ATTN_TILE == 0
    tm = 512

    def vec(a):
        return a.reshape(L, 1, a.shape[-1])

    f1 = (ffn1_w1, ffn1_w3, ffn1_w2)
    f2 = (ffn2_w1, ffn2_w3, ffn2_w2)
    w_in_b = w_in.astype(_BF)
    w_branch_b = w_branch.astype(_BF)
    w_out_b = w_out.astype(_BF)
    ple_gate_b = ple_gate_w.astype(_BF)
    ple_proj_b = ple_proj.astype(_BF)
    pool_w_b = pool_w.astype(_BF)
    wa_bd, wx_bd = _block_diag(rg_wa), _block_diag(rg_wx)
    ones = jnp.ones((L, COL_Q), _F32)
    qk_gain = jnp.concatenate(
        [ones, jnp.tile(q_gain, (1, (COL_K - COL_Q) // HEAD_DIM)),
         jnp.tile(k_gain, (1, (COL_V - COL_K) // HEAD_DIM)),
         jnp.ones((L, MIX_COLS - COL_V), _F32)], axis=1).reshape(L, 1, MIX_COLS)
    p2 = p.reshape(L, T, PLE_DIM)
    sg_b4 = sg_b.reshape(L, SG_GROUPS, SG_CHUNK, 1)

    x2 = x.reshape(T, D)
    for i in range(L):
        x2 = _ffn(x2, vec(ffn1_norm), *f1, i, 1024)
        proj, qkv = _proj(x2, vec(mix_norm), w_in_b, qk_gain, i, 1024, 1024)
        ya = _rglru(proj, conv_w, vec(conv_b), wa_bd, vec(rg_ba), wx_bd, vec(rg_bx), vec(rg_lambda),
                    i, B, S, 256)
        yb = _pool(proj, pool_w_b, vec(pool_scale), i, B, S, 512)
        yc = _attn(qkv, B, S)
        yd = _sgu(proj, vec(sg_norm), sg_w, sg_b4, i, T, 512)
        x2 = _merge(x2, vec(mix_norm), (ya, yb, yc, yd), w_in_b, b_gate, w_branch_b, w_out_b, i, tm, 256)
        x2 = _ffn(x2, vec(ffn2_norm), *f2, i, 1024)
        x2 = _ple(x2, vec(ple_norm), p2, ple_gate_b, ple_proj_b, i, 1024, PLE_TILE)
    return x2.reshape(B, S, D)
```

```python
import functools

import numpy as np

import jax
import jax.numpy as jnp
from jax import lax
from jax.experimental import pallas as pl
from jax.experimental.pallas import tpu as pltpu

EPS = 1e-6
NEG_INF = -1e30
D_MODEL = 2048
D_FF = 5504
FF_TILE = 512
PLE_TILE = 1024
PLE_DIM = 256
WIDTH = 1024
RNN_HEADS = 16
RNN_HEAD_DIM = 64
RNN_BLOCK = 256
RG_C = 8.0
CONV_WIDTH = 4
POOL_WINDOWS = (2, 4, 8, 16)
POOL_GROUP = 256
POOL_HALO = 16
CONV_HALO = 8
SCAN_GROUP = 8
HEAD_DIM = 128
KV_HEADS = 8
ATTN_DILATIONS = (1, 4, 16)
ATTN_BLOCK = 128
ATTN_TILE = ATTN_BLOCK * max(ATTN_DILATIONS)
SG_CHUNK = 128
SG_GROUPS = 8
N_BRANCH = 4
COL_XA, COL_GA, COL_XB, COL_Q, COL_K, COL_V, COL_ZD, COL_GATES = (
    0, 1024, 2048, 3072, 6144, 7168, 8192, 10240)
MIX_COLS = COL_GATES
NAT_ZD, NAT_XA, NAT_GA, NAT_XB, NAT_COLS = 0, 2048, 3072, 4096, 5120
QKV_Q, QKV_K, QKV_V, QKV_COLS = 0, 3072, 4096, 5120
ATTN_SLABS = 16
PERM_ROWS = 256
VMEM_LIMIT = 56 * 1024 * 1024

_BF = jnp.bfloat16
_F32 = jnp.float32


def _params(*sem):
    return pltpu.CompilerParams(dimension_semantics=sem, vmem_limit_bytes=VMEM_LIMIT)


def _rms(x, g):
    return x * lax.rsqrt(jnp.mean(x * x, axis=-1, keepdims=True) + EPS) * g


def _dot(a, b):
    return jnp.dot(a, b, preferred_element_type=_F32)


def _x_spec(tm, d, n_tiles):
    last = n_tiles - 1
    return pl.BlockSpec((tm, d), lambda i, j: (jnp.where(j == 0, i, jnp.minimum(i + 1, last)), 0))


def _ffn_kernel(x_ref, g_ref, w1_ref, w3_ref, w2_ref, o_ref, h_ref, *, nf):
    j = pl.program_id(1)
    tf = w1_ref.shape[1]

    @pl.when(j == 0)
    def _():
        h_ref[...] = _rms(x_ref[...], g_ref[...]).astype(_BF)

    def ffn_step(valid, first):
        h = h_ref[...]
        a = _dot(h, w1_ref[:, :valid])
        b = _dot(h, w3_ref[:, :valid])
        act = (0.5 * (a * jax.nn.sigmoid(a)) * b).astype(_BF)
        for c in range(o_ref.shape[1] // FF_TILE):
            sl = slice(c * FF_TILE, (c + 1) * FF_TILE)
            base = x_ref[:, sl] if first else o_ref[:, sl]
            o_ref[:, sl] = base + _dot(act, w2_ref[:valid, sl])

    last_valid = D_FF - (nf - 1) * tf
    pl.when(j == 0)(lambda: ffn_step(tf, True))
    pl.when(jnp.logical_and(j > 0, j < nf - 1))(lambda: ffn_step(tf, False))
    pl.when(j == nf - 1)(lambda: ffn_step(last_valid, False))


def _ffn(x2, norm, w1, w3, w2, layer, tm):
    T, D = x2.shape
    tf = FF_TILE
    nf = -(-D_FF // tf)
    return pl.pallas_call(
        functools.partial(_ffn_kernel, nf=nf),
        grid=(T // tm, nf),
        in_specs=[
            _x_spec(tm, D, T // tm),
            pl.BlockSpec((None, 1, D), lambda i, j: (layer, 0, 0)),
            pl.BlockSpec((None, D, tf), lambda i, j: (layer, 0, j)),
            pl.BlockSpec((None, D, tf), lambda i, j: (layer, 0, j)),
            pl.BlockSpec((None, tf, D), lambda i, j: (layer, j, 0)),
        ],
        out_specs=pl.BlockSpec((tm, D), lambda i, j: (i, 0)),
        out_shape=jax.ShapeDtypeStruct((T, D), _F32),
        scratch_shapes=[pltpu.VMEM((tm, D), _BF)],
        compiler_params=_params("parallel", "arbitrary"),
        name="ffn",
    )(x2, norm, w1, w3, w2)


def _ple_kernel(x_ref, g_ref, p_ref, wg_ref, wp_ref, o_ref, h_ref):
    j = pl.program_id(1)

    @pl.when(j == 0)
    def _():
        h_ref[...] = _rms(x_ref[...], g_ref[...]).astype(_BF)

    gate = jax.nn.sigmoid(_dot(h_ref[...], wg_ref[...]))
    tn = o_ref.shape[1]
    cols = pl.ds(pl.multiple_of(j * tn, tn), tn)
    o_ref[...] = x_ref[:, cols] + gate * _dot(p_ref[...].astype(_BF), wp_ref[...])


def _ple(x2, norm, p, w_gate, w_proj, layer, tm, tn):
    T, D = x2.shape
    return pl.pallas_call(
        _ple_kernel,
        grid=(T // tm, D // tn),
        in_specs=[
            pl.BlockSpec((tm, D), lambda i, j: (i, 0)),
            pl.BlockSpec((None, 1, D), lambda i, j: (layer, 0, 0)),
            pl.BlockSpec((None, tm, PLE_DIM), lambda i, j: (layer, i, 0)),
            pl.BlockSpec((None, D, tn), lambda i, j: (layer, 0, j)),
            pl.BlockSpec((None, PLE_DIM, tn), lambda i, j: (layer, 0, j)),
        ],
        out_specs=pl.BlockSpec((tm, tn), lambda i, j: (i, j)),
        out_shape=jax.ShapeDtypeStruct((T, D), _F32),
        scratch_shapes=[pltpu.VMEM((tm, D), _BF)],
        compiler_params=_params("parallel", "arbitrary"),
        name="ple",
    )(x2, norm, p, w_gate, w_proj)


def _slab_of_residue(r):
    return (r % 4) * 4 + r // 4


def _proj_kernel(x_ref, g_ref, w_ref, gain_ref, nat_ref, qkv_ref, h_ref, hp_ref, *, n_nat, n_norm):
    j = pl.program_id(1)
    tm = x_ref.shape[0]
    nsub = tm // PERM_ROWS

    @pl.when(j == 0)
    def _():
        h = _rms(x_ref[...], g_ref[...]).astype(_BF)
        h_ref[...] = h
        lam = lax.broadcasted_iota(jnp.int32, (PERM_ROWS, PERM_ROWS), 0)
        tau = lax.broadcasted_iota(jnp.int32, (PERM_ROWS, PERM_ROWS), 1)
        r = tau % ATTN_SLABS
        perm = (lam == _slab_of_residue(r) * (PERM_ROWS // ATTN_SLABS) + tau // ATTN_SLABS).astype(_BF)
        for s in range(nsub):
            rows = slice(s * PERM_ROWS, (s + 1) * PERM_ROWS)
            hp_ref[rows, :] = _dot(perm, h[rows, :]).astype(_BF)

    @pl.when(j < n_nat)
    def _():
        nat_ref[...] = _dot(h_ref[...], w_ref[...])

    def store_qkv(cols, y):
        run = PERM_ROWS // ATTN_SLABS
        for s in range(nsub):
            for slab in range(ATTN_SLABS):
                r0 = s * PERM_ROWS + slab * run
                qkv_ref[slab, s * run:(s + 1) * run, cols] = y[r0:r0 + run, :]

    is_qk = jnp.logical_and(j >= n_nat, j < n_nat + n_norm)

    @pl.when(is_qk)
    def _():
        acc = _dot(hp_ref[...], w_ref[...])
        for c in range(acc.shape[1] // HEAD_DIM):
            sl = slice(c * HEAD_DIM, (c + 1) * HEAD_DIM)
            store_qkv(sl, _rms(acc[:, sl], gain_ref[:, sl]))

    @pl.when(j >= n_nat + n_norm)
    def _():
        store_qkv(slice(None), _dot(hp_ref[...], w_ref[...]))


def _proj(x2, norm, w_in, gain, layer, tm, tn):
    T, D = x2.shape
    n_zd = (COL_GATES - COL_ZD) // tn
    n_nat = n_zd + COL_Q // tn
    n_norm = (COL_V - COL_Q) // tn
    n_all = MIX_COLS // tn
    assert tm % PERM_ROWS == 0 and ATTN_TILE % tm == 0
    per_tile = ATTN_TILE // tm

    def w_col(j):
        return jnp.where(j < n_zd, j + COL_ZD // tn, j - n_zd)

    return pl.pallas_call(
        functools.partial(_proj_kernel, n_nat=n_nat, n_norm=n_norm),
        grid=(T // tm, n_all),
        in_specs=[
            _x_spec(tm, D, T // tm),
            pl.BlockSpec((None, 1, D), lambda i, j: (layer, 0, 0)),
            pl.BlockSpec((None, D, tn), lambda i, j: (layer, 0, w_col(j))),
            pl.BlockSpec((None, 1, tn), lambda i, j: (layer, 0, w_col(j))),
        ],
        out_specs=[
            pl.BlockSpec((tm, tn), lambda i, j: (i, jnp.minimum(j, n_nat - 1))),
            pl.BlockSpec((None, ATTN_SLABS, tm // ATTN_SLABS, tn),
                         lambda i, j: (i // per_tile, 0, i % per_tile, jnp.maximum(j - n_nat, 0))),
        ],
        out_shape=[jax.ShapeDtypeStruct((T, NAT_COLS), _F32),
                   jax.ShapeDtypeStruct((T // ATTN_TILE, ATTN_SLABS, ATTN_BLOCK, QKV_COLS), _F32)],
        scratch_shapes=[pltpu.VMEM((tm, D), _BF), pltpu.VMEM((tm, D), _BF)],
        compiler_params=_params("parallel", "arbitrary"),
        name="proj",
    )(x2, norm, w_in, gain)


def _rglru_kernel(xa_ref, halo_ref, ga_ref, cw_ref, cb_ref, wa_ref, ba_ref, wx_ref, bx_ref,
                  lam_ref, o_ref, carry_ref, sa_ref, sb_ref, xe_ref, *, tb):
    i = pl.program_id(1)

    @pl.when(i == 0)
    def _():
        carry_ref[...] = jnp.zeros_like(carry_ref)

    xe_ref[:CONV_HALO, :] = jnp.where(i == 0, 0.0, halo_ref[...])
    xe_ref[CONV_HALO:, :] = xa_ref[...]
    cw = cw_ref[...]
    xc = cb_ref[...] + cw[0:1] * xa_ref[...]
    for j in range(1, CONV_WIDTH):
        xc = xc + cw[j:j + 1] * xe_ref[pl.ds(CONV_HALO - j, tb), :]
    xcb = xc.astype(_BF)
    r_lin, i_lin = [], []
    for p in range(WIDTH // RNN_BLOCK):
        sl = slice(p * RNN_BLOCK, (p + 1) * RNN_BLOCK)
        r_lin.append(_dot(xcb[:, sl], wa_ref[p]))
        i_lin.append(_dot(xcb[:, sl], wx_ref[p]))
    r = jax.nn.sigmoid(jnp.concatenate(r_lin, axis=1) + ba_ref[...])
    ig = jax.nn.sigmoid(jnp.concatenate(i_lin, axis=1) + bx_ref[...])
    z = -lam_ref[...]
    softplus = jnp.maximum(z, 0.0) + jnp.log1p(jnp.exp(-jnp.abs(z)))
    log_a = (-RG_C * r) * softplus
    a = jnp.exp(log_a)
    y = 1.0 - a * a
    b = jnp.where(y > 0.0, y * lax.rsqrt(y), 0.0) * (ig * xc)
    ng = tb // SCAN_GROUP
    a = a.reshape(ng, SCAN_GROUP, WIDTH)
    b = b.reshape(ng, SCAN_GROUP, WIDTH)
    row = lax.broadcasted_iota(jnp.int32, (1, SCAN_GROUP, 1), 1)
    s = 1
    while s < SCAN_GROUP:
        keep = row >= s
        a_prev = jnp.where(keep, pltpu.roll(a, s, axis=1), 1.0)
        b_prev = jnp.where(keep, pltpu.roll(b, s, axis=1), 0.0)
        b = a * b_prev + b
        a = a * a_prev
        s *= 2
    sa_ref[...] = a
    sb_ref[...] = b

    def chain(g, carry):
        hg = sa_ref[g] * carry + sb_ref[g]
        sb_ref[g] = hg
        return hg[SCAN_GROUP - 1:SCAN_GROUP, :]

    carry_ref[0:1, :] = lax.fori_loop(0, ng, chain, carry_ref[0:1, :], unroll=8)
    h = sb_ref[...].reshape(tb, WIDTH)
    o_ref[...] = (h * jax.nn.gelu(ga_ref[...])).astype(_BF)


def _rglru(proj, cw, cb, wa, ba, wx, bx, lam, layer, B, S, tb):
    nb = S // tb
    hb = tb // CONV_HALO
    vec = pl.BlockSpec((None, 1, WIDTH), lambda b, i: (layer, 0, 0))
    mat = pl.BlockSpec((None, WIDTH // RNN_BLOCK, RNN_BLOCK, RNN_BLOCK), lambda b, i: (layer, 0, 0, 0))
    return pl.pallas_call(
        functools.partial(_rglru_kernel, tb=tb),
        grid=(B, nb),
        in_specs=[
            pl.BlockSpec((tb, WIDTH), lambda b, i: (b * nb + i, NAT_XA // WIDTH)),
            pl.BlockSpec((CONV_HALO, WIDTH),
                         lambda b, i: (jnp.maximum((b * nb + i) * hb - 1, 0), NAT_XA // WIDTH)),
            pl.BlockSpec((tb, WIDTH), lambda b, i: (b * nb + i, NAT_GA // WIDTH)),
            pl.BlockSpec((None, CONV_WIDTH, WIDTH), lambda b, i: (layer, 0, 0)),
            vec, mat, vec, mat, vec, vec,
        ],
        out_specs=pl.BlockSpec((tb, WIDTH), lambda b, i: (b * nb + i, 0)),
        out_shape=jax.ShapeDtypeStruct((B * S, WIDTH), _BF),
        scratch_shapes=[pltpu.VMEM((8, WIDTH), _F32),
                        pltpu.VMEM((tb // SCAN_GROUP, SCAN_GROUP, WIDTH), _F32),
                        pltpu.VMEM((tb // SCAN_GROUP, SCAN_GROUP, WIDTH), _F32),
                        pltpu.VMEM((tb + CONV_HALO, WIDTH), _F32)],
        compiler_params=_params("parallel", "arbitrary"),
        name="rglru",
    )(proj, proj, proj, cw, cb, wa, ba, wx, bx, lam)


def _pool_kernel(xb_ref, halo_ref, pw_ref, sc_ref, o_ref, *, tb):
    i = pl.program_id(1)
    x = xb_ref[...]
    halo = jnp.where(i == 0, 0.0, halo_ref[...])
    xe = jnp.concatenate([halo, x], axis=0)
    pos = i * tb + lax.broadcasted_iota(jnp.int32, (tb, 1), 0)
    for g, win in enumerate(POOL_WINDOWS):
        sl = slice(g * POOL_GROUP, (g + 1) * POOL_GROUP)
        s = xe[:, sl]
        sh = 1
        while sh < win:
            s = s + pltpu.roll(s, sh, axis=0)
            sh *= 2
        cnt = jnp.minimum(pos + 1, win).astype(_F32)
        pooled = s[POOL_HALO:] / cnt - x[:, sl]
        y = _dot(pooled.astype(_BF), pw_ref[g])
        o_ref[:, sl] = (y * sc_ref[:, sl]).astype(_BF)


def _pool(proj, pw, sc, layer, B, S, tb):
    nb = S // tb
    hb = tb // POOL_HALO
    return pl.pallas_call(
        functools.partial(_pool_kernel, tb=tb),
        grid=(B, nb),
        in_specs=[
            pl.BlockSpec((tb, WIDTH), lambda b, i: (b * nb + i, NAT_XB // WIDTH)),
            pl.BlockSpec((POOL_HALO, WIDTH),
                         lambda b, i: (jnp.maximum((b * nb + i) * hb - 1, 0), NAT_XB // WIDTH)),
            pl.BlockSpec((None, len(POOL_WINDOWS), POOL_GROUP, POOL_GROUP), lambda b, i: (layer, 0, 0, 0)),
            pl.BlockSpec((None, 1, WIDTH), lambda b, i: (layer, 0, 0)),
        ],
        out_specs=pl.BlockSpec((tb, WIDTH), lambda b, i: (b * nb + i, 0)),
        out_shape=jax.ShapeDtypeStruct((B * S, WIDTH), _BF),
        compiler_params=_params("parallel", "parallel"),
        name="pool",
    )(proj, proj, pw, sc)


def _attn_order(g):
    slab_res = np.empty(ATTN_SLABS, np.int64)
    for r in range(ATTN_SLABS):
        slab_res[_slab_of_residue(r)] = r
    if g == 0:
        qi = (slab_res[:, None] + ATTN_SLABS * np.arange(8)[None, :]).reshape(-1)
        ki = (slab_res[:, None] + ATTN_SLABS * np.arange(16)[None, :]).reshape(-1)
    elif g == 1:
        qi = (np.arange(4)[:, None] + 4 * np.arange(32)[None, :]).reshape(-1)
        ki = (np.arange(4)[:, None] + 4 * np.arange(64)[None, :]).reshape(-1)
    else:
        qi = np.arange(ATTN_BLOCK)
        ki = np.arange(2 * ATTN_BLOCK)
    return qi, ki


def _attn_bias():
    out = np.empty((2 * len(ATTN_DILATIONS), ATTN_BLOCK, 2 * ATTN_BLOCK), np.float32)
    for g in range(len(ATTN_DILATIONS)):
        qi, ki = _attn_order(g)
        band = (ki[None, :] >= qi[:, None]) & (ki[None, :] <= qi[:, None] + ATTN_BLOCK)
        out[2 * g] = np.where(band, 0.0, NEG_INF)
        out[2 * g + 1] = np.where(band & (ki[None, :] >= ATTN_BLOCK), 0.0, NEG_INF)
    return out


def _attn_kernel(q0_ref, q1_ref, q2_ref, kc_ref, kp_ref, vc_ref, vp_ref, bias_ref, o_ref,
                 o0_ref, o1_ref, o2_ref, l0_ref, l1_ref, l2_ref):
    n = pl.program_id(2)
    NJ = ATTN_BLOCK
    scale = HEAD_DIM ** -0.5
    exp2_scale = scale * 1.4426950408889634
    is_first_tile = (n == 0).astype(jnp.int32)

    shapes = ((ATTN_SLABS, 8), (4, 32), (1, ATTN_BLOCK))
    for g, (q_ref, og_ref, lg_ref) in enumerate(((q0_ref, o0_ref, l0_ref), (q1_ref, o1_ref, l1_ref),
                                                 (q2_ref, o2_ref, l2_ref))):
        ns, nj = shapes[g]
        classes = ATTN_SLABS // ns
        for idx in range(ATTN_TILE // ATTN_BLOCK):
            c = idx % classes
            m = idx // classes
            slabs = slice(c * ns, (c + 1) * ns)
            js = slice(m * nj, (m + 1) * nj)
            q = q_ref[slabs, js, :].reshape(ATTN_BLOCK, HEAD_DIM).astype(_BF)
            if m == 0:
                tail = slice(NJ - nj, NJ)
                k = jnp.concatenate([kp_ref[slabs, tail, :], kc_ref[slabs, js, :]], axis=1)
                v = jnp.concatenate([vp_ref[slabs, tail, :], vc_ref[slabs, js, :]], axis=1)
                bias = bias_ref[2 * g + is_first_tile]
            else:
                both = slice((m - 1) * nj, (m + 1) * nj)
                k = kc_ref[slabs, both, :]
                v = vc_ref[slabs, both, :]
                bias = bias_ref[2 * g]
            k = k.reshape(2 * ATTN_BLOCK, HEAD_DIM).astype(_BF)
            v = v.reshape(2 * ATTN_BLOCK, HEAD_DIM).astype(_BF)
            s = lax.dot_general(q, k, (((1,), (1,)), ((), ())), preferred_element_type=_F32) + bias
            mx = jnp.max(s, axis=-1, keepdims=True)
            p = jnp.exp2((s - mx) * exp2_scale)
            l = jnp.sum(p, axis=-1, keepdims=True)
            o = _dot(p.astype(_BF), v) / l
            lse = mx * scale + jnp.log(l)
            og_ref[slabs, js, :] = o.reshape(ns, nj, HEAD_DIM)
            lg_ref[slabs, js, :] = jnp.broadcast_to(lse, (ATTN_BLOCK, HEAD_DIM)).reshape(ns, nj, HEAD_DIM)

    l0, l1, l2 = l0_ref[...], l1_ref[...], l2_ref[...]
    mx = jnp.maximum(jnp.maximum(l0, l1), l2)
    w0, w1, w2 = jnp.exp(l0 - mx), jnp.exp(l1 - mx), jnp.exp(l2 - mx)
    out = (w0 * o0_ref[...] + w1 * o1_ref[...] + w2 * o2_ref[...]) / (w0 + w1 + w2)
    for r in range(ATTN_SLABS):
        o_ref[pl.ds(r, NJ, stride=ATTN_SLABS), :] = out[_slab_of_residue(r)]


def _attn(qkv, B, S):
    nt = S // ATTN_TILE
    qb, kb, vb = QKV_Q // HEAD_DIM, QKV_K // HEAD_DIM, QKV_V // HEAD_DIM
    blk = (None, ATTN_SLABS, ATTN_BLOCK, HEAD_DIM)

    def cur(col0):
        return pl.BlockSpec(blk, lambda b, h, n: (b * nt + n, 0, 0, col0 + h))

    def prev(col0):
        return pl.BlockSpec(blk, lambda b, h, n: (b * nt + jnp.maximum(n - 1, 0), 0, 0, col0 + h))

    n_bias = 2 * len(ATTN_DILATIONS)
    tile = pltpu.VMEM((ATTN_SLABS, ATTN_BLOCK, HEAD_DIM), _F32)
    return pl.pallas_call(
        _attn_kernel,
        grid=(B, KV_HEADS, nt),
        in_specs=[cur(qb), cur(qb + KV_HEADS), cur(qb + 2 * KV_HEADS),
                  cur(kb), prev(kb), cur(vb), prev(vb),
                  pl.BlockSpec((n_bias, ATTN_BLOCK, 2 * ATTN_BLOCK), lambda b, h, n: (0, 0, 0))],
        out_specs=pl.BlockSpec((ATTN_TILE, HEAD_DIM), lambda b, h, n: (b * nt + n, h)),
        out_shape=jax.ShapeDtypeStruct((B * S, WIDTH), _F32),
        scratch_shapes=[tile, tile, tile, tile, tile, tile],
        compiler_params=_params("parallel", "parallel", "parallel"),
        name="attn",
    )(qkv, qkv, qkv, qkv, qkv, qkv, qkv, jnp.asarray(_attn_bias()))


def _sgu_kernel(z_ref, g_ref, ws_ref, b_ref, o_ref, *, tb):
    gz = jax.nn.gelu(z_ref[...])
    u = gz[:, :WIDTH]
    vv = _rms(gz[:, WIDTH:], g_ref[...]).astype(_BF)
    row = lax.broadcasted_iota(jnp.int32, (SG_CHUNK, SG_CHUNK), 0)
    col = lax.broadcasted_iota(jnp.int32, (SG_CHUNK, SG_CHUNK), 1)
    tri = row >= col
    for g in range(SG_GROUPS):
        w = jnp.where(tri, ws_ref[g], 0.0).astype(_BF)
        cs = slice(g * SG_CHUNK, (g + 1) * SG_CHUNK)
        for c in range(tb // SG_CHUNK):
            rs = slice(c * SG_CHUNK, (c + 1) * SG_CHUNK)
            mixed = _dot(w, vv[rs, cs]) + b_ref[g]
            o_ref[rs, cs] = (u[rs, cs] * mixed).astype(_BF)


def _sgu(proj, sg_norm, sg_w, sg_b, layer, T, tb):
    return pl.pallas_call(
        functools.partial(_sgu_kernel, tb=tb),
        grid=(T // tb,),
        in_specs=[
            pl.BlockSpec((tb, 2 * WIDTH), lambda i: (i, NAT_ZD // (2 * WIDTH))),
            pl.BlockSpec((None, 1, WIDTH), lambda i: (layer, 0, 0)),
            pl.BlockSpec((None, SG_GROUPS, SG_CHUNK, SG_CHUNK), lambda i: (layer, 0, 0, 0)),
            pl.BlockSpec((None, SG_GROUPS, SG_CHUNK, 1), lambda i: (layer, 0, 0, 0)),
        ],
        out_specs=pl.BlockSpec((tb, WIDTH), lambda i: (i, 0)),
        out_shape=jax.ShapeDtypeStruct((T, WIDTH), _BF),
        compiler_params=_params("parallel"),
        name="sgu",
    )(proj, sg_norm, sg_w, sg_b)


def _merge_kernel(x_ref, g_ref, ya_ref, yb_ref, yc_ref, yd_ref, wg0_ref, wg1_ref, wg2_ref, wg3_ref,
                  bg_ref, wb_ref, wo_ref, o_ref, h_ref):
    j = pl.program_id(1)

    @pl.when(j == 0)
    def _():
        h_ref[...] = _rms(x_ref[...], g_ref[...]).astype(_BF)

    def step(first):
        h = h_ref[...]
        merged = None
        for b, (y_ref, wg_ref) in enumerate(((ya_ref, wg0_ref), (yb_ref, wg1_ref),
                                             (yc_ref, wg2_ref), (yd_ref, wg3_ref))):
            gate = jax.nn.sigmoid(_dot(h, wg_ref[...]) + bg_ref[b:b + 1, :])
            term = gate * _dot(y_ref[...].astype(_BF), wb_ref[b])
            merged = term if merged is None else merged + term
        base = x_ref[...] if first else o_ref[...]
        o_ref[...] = base + _dot(merged.astype(_BF), wo_ref[...])

    pl.when(j == 0)(lambda: step(True))
    pl.when(j > 0)(lambda: step(False))


def _merge(x2, norm, ys, w_in, b_gate, w_branch, w_out, layer, tm, tn):
    T, D = x2.shape
    y_spec = pl.BlockSpec((tm, WIDTH), lambda i, j: (i, 0))

    def gate_spec(b):
        col0 = (COL_GATES + b * D) // tn
        return pl.BlockSpec((None, D, tn), lambda i, j: (layer, 0, col0 + j))

    return pl.pallas_call(
        _merge_kernel,
        grid=(T // tm, D // tn),
        in_specs=[
            _x_spec(tm, D, T // tm),
            pl.BlockSpec((None, 1, D), lambda i, j: (layer, 0, 0)),
            y_spec, y_spec, y_spec, y_spec,
            gate_spec(0), gate_spec(1), gate_spec(2), gate_spec(3),
            pl.BlockSpec((None, N_BRANCH, tn), lambda i, j: (layer, 0, j)),
            pl.BlockSpec((None, N_BRANCH, WIDTH, tn), lambda i, j: (layer, 0, 0, j)),
            pl.BlockSpec((None, tn, D), lambda i, j: (layer, j, 0)),
        ],
        out_specs=pl.BlockSpec((tm, D), lambda i, j: (i, 0)),
        out_shape=jax.ShapeDtypeStruct((T, D), _F32),
        scratch_shapes=[pltpu.VMEM((tm, D), _BF)],
        compiler_params=_params("parallel", "arbitrary"),
        name="merge",
    )(x2, norm, *ys, w_in, w_in, w_in, w_in, b_gate, w_branch, w_out)


def _block_diag(w):
    L = w.shape[0]
    per = RNN_BLOCK // RNN_HEAD_DIM
    w = w.reshape(L, RNN_HEADS // per, per, RNN_HEAD_DIM, RNN_HEAD_DIM)
    eye = jnp.eye(per, dtype=w.dtype)
    bd = jnp.einsum('lphij,hk->lphikj', w, eye)
    return bd.reshape(L, RNN_HEADS // per, RNN_BLOCK, RNN_BLOCK).astype(_BF)


def kernel(x, p, ffn1_norm, ffn1_w1, ffn1_w3, ffn1_w2, mix_norm, w_in, b_gate, conv_w, conv_b, rg_wa, rg_ba, rg_wx, rg_bx, rg_lambda, pool_w, pool_scale, q_gain, k_gain, sg_norm, sg_w, sg_b, w_branch, w_out, ffn2_norm, ffn2_w1, ffn2_w3, ffn2_w2, ple_norm, ple_gate_w, ple_proj):
    B, S, D = x.shape
    L = w_in.shape[0]
    T = B * S
    assert D == D_MODEL and S % ATTN_TILE == 0
    tm = 512

    def vec(a):
        return a.reshape(L, 1, a.shape[-1])

    f1 = (ffn1_w1.astype(_BF), ffn1_w3.astype(_BF), ffn1_w2.astype(_BF))
    f2 = (ffn2_w1.astype(_BF), ffn2_w3.astype(_BF), ffn2_w2.astype(_BF))
    w_in_b = w_in.astype(_BF)
    w_branch_b = w_branch.astype(_BF)
    w_out_b = w_out.astype(_BF)
    ple_gate_b = ple_gate_w.astype(_BF)
    ple_proj_b = ple_proj.astype(_BF)
    pool_w_b = pool_w.astype(_BF)
    wa_bd, wx_bd = _block_diag(rg_wa), _block_diag(rg_wx)
    ones = jnp.ones((L, COL_Q), _F32)
    qk_gain = jnp.concatenate(
        [ones, jnp.tile(q_gain, (1, (COL_K - COL_Q) // HEAD_DIM)),
         jnp.tile(k_gain, (1, (COL_V - COL_K) // HEAD_DIM)),
         jnp.ones((L, MIX_COLS - COL_V), _F32)], axis=1).reshape(L, 1, MIX_COLS)
    p2 = p.reshape(L, T, PLE_DIM)
    sg_b4 = sg_b.reshape(L, SG_GROUPS, SG_CHUNK, 1)

    x2 = x.reshape(T, D)
    for i in range(L):
        x2 = _ffn(x2, vec(ffn1_norm), *f1, i, 1024)
        proj, qkv = _proj(x2, vec(mix_norm), w_in_b, qk_gain, i, 1024, 1024)
        ya = _rglru(proj, conv_w, vec(conv_b), wa_bd, vec(rg_ba), wx_bd, vec(rg_bx), vec(rg_lambda),
                    i, B, S, 256)
        yb = _pool(proj, pool_w_b, vec(pool_scale), i, B, S, 512)
        yc = _attn(qkv, B, S)
        yd = _sgu(proj, vec(sg_norm), sg_w, sg_b4, i, T, 512)
        x2 = _merge(x2, vec(mix_norm), (ya, yb, yc, yd), w_in_b, b_gate, w_branch_b, w_out_b, i, tm, 256)
        x2 = _ffn(x2, vec(ffn2_norm), *f2, i, 1024)
        x2 = _ple(x2, vec(ple_norm), p2, ple_gate_b, ple_proj_b, i, 1024, PLE_TILE)
    return x2.reshape(B, S, D)
```

```python
import functools

import numpy as np

import jax
import jax.numpy as jnp
from jax import lax
from jax.experimental import pallas as pl
from jax.experimental.pallas import tpu as pltpu

EPS = 1e-6
NEG_INF = -1e30
D_MODEL = 2048
D_FF = 5504
FF_TILE = 512
PLE_TILE = 1024
PLE_DIM = 256
WIDTH = 1024
RNN_HEADS = 16
RNN_HEAD_DIM = 64
RNN_BLOCK = 256
RG_C = 8.0
CONV_WIDTH = 4
POOL_WINDOWS = (2, 4, 8, 16)
POOL_GROUP = 256
POOL_HALO = 16
CONV_HALO = 8
SCAN_GROUP = 8
HEAD_DIM = 128
KV_HEADS = 8
ATTN_DILATIONS = (1, 4, 16)
ATTN_BLOCK = 128
ATTN_TILE = ATTN_BLOCK * max(ATTN_DILATIONS)
SG_CHUNK = 128
SG_GROUPS = 8
N_BRANCH = 4
COL_XA, COL_GA, COL_XB, COL_Q, COL_K, COL_V, COL_ZD, COL_GATES = (
    0, 1024, 2048, 3072, 6144, 7168, 8192, 10240)
MIX_COLS = COL_GATES
NAT_ZD, NAT_XA, NAT_GA, NAT_XB, NAT_COLS = 0, 2048, 3072, 4096, 5120
QKV_Q, QKV_K, QKV_V, QKV_COLS = 0, 3072, 4096, 5120
ATTN_SLABS = 16
PERM_ROWS = 256
VMEM_LIMIT = 56 * 1024 * 1024

_BF = jnp.bfloat16
_F32 = jnp.float32


def _params(*sem):
    return pltpu.CompilerParams(dimension_semantics=sem, vmem_limit_bytes=VMEM_LIMIT)


def _rms(x, g):
    return x * lax.rsqrt(jnp.mean(x * x, axis=-1, keepdims=True) + EPS) * g


def _dot(a, b):
    return jnp.dot(a, b, preferred_element_type=_F32)


def _x_spec(tm, d, n_tiles):
    last = n_tiles - 1
    return pl.BlockSpec((tm, d), lambda i, j: (jnp.where(j == 0, i, jnp.minimum(i + 1, last)), 0))


def _ffn_kernel(x_ref, g_ref, w1_ref, w3_ref, w2_ref, o_ref, h_ref, *, nf):
    j = pl.program_id(1)
    tf = w1_ref.shape[1]

    @pl.when(j == 0)
    def _():
        h_ref[...] = _rms(x_ref[...], g_ref[...]).astype(_BF)

    def ffn_step(valid, first):
        h = h_ref[...]
        a = _dot(h, w1_ref[:, :valid])
        b = _dot(h, w3_ref[:, :valid])
        act = (0.5 * (a * jax.nn.sigmoid(a)) * b).astype(_BF)
        for c in range(o_ref.shape[1] // FF_TILE):
            sl = slice(c * FF_TILE, (c + 1) * FF_TILE)
            base = x_ref[:, sl] if first else o_ref[:, sl]
            o_ref[:, sl] = base + _dot(act, w2_ref[:valid, sl])

    last_valid = D_FF - (nf - 1) * tf
    pl.when(j == 0)(lambda: ffn_step(tf, True))
    pl.when(jnp.logical_and(j > 0, j < nf - 1))(lambda: ffn_step(tf, False))
    pl.when(j == nf - 1)(lambda: ffn_step(last_valid, False))


def _ffn(x2, norm, w1, w3, w2, layer, tm):
    T, D = x2.shape
    tf = FF_TILE
    nf = -(-D_FF // tf)
    return pl.pallas_call(
        functools.partial(_ffn_kernel, nf=nf),
        grid=(T // tm, nf),
        in_specs=[
            _x_spec(tm, D, T // tm),
            pl.BlockSpec((None, 1, D), lambda i, j: (layer, 0, 0)),
            pl.BlockSpec((None, D, tf), lambda i, j: (layer, 0, j)),
            pl.BlockSpec((None, D, tf), lambda i, j: (layer, 0, j)),
            pl.BlockSpec((None, tf, D), lambda i, j: (layer, j, 0)),
        ],
        out_specs=pl.BlockSpec((tm, D), lambda i, j: (i, 0)),
        out_shape=jax.ShapeDtypeStruct((T, D), _F32),
        scratch_shapes=[pltpu.VMEM((tm, D), _BF)],
        compiler_params=_params("parallel", "arbitrary"),
        name="ffn",
    )(x2, norm, w1, w3, w2)


def _ple_kernel(x_ref, g_ref, p_ref, wg_ref, wp_ref, o_ref, h_ref):
    j = pl.program_id(1)

    tn = o_ref.shape[1]

    def step(h, cols):
        gate = jax.nn.sigmoid(_dot(h, wg_ref[...]))
        o_ref[...] = x_ref[:, cols] + gate * _dot(p_ref[...].astype(_BF), wp_ref[...])

    @pl.when(j == 0)
    def _():
        h = _rms(x_ref[...], g_ref[...]).astype(_BF)
        h_ref[...] = h
        step(h, slice(0, tn))

    @pl.when(j > 0)
    def _():
        step(h_ref[...], pl.ds(pl.multiple_of(j * tn, tn), tn))


def _ple(x2, norm, p, w_gate, w_proj, layer, tm, tn):
    T, D = x2.shape
    return pl.pallas_call(
        _ple_kernel,
        grid=(T // tm, D // tn),
        in_specs=[
            pl.BlockSpec((tm, D), lambda i, j: (i, 0)),
            pl.BlockSpec((None, 1, D), lambda i, j: (layer, 0, 0)),
            pl.BlockSpec((None, tm, PLE_DIM), lambda i, j: (layer, i, 0)),
            pl.BlockSpec((None, D, tn), lambda i, j: (layer, 0, j)),
            pl.BlockSpec((None, PLE_DIM, tn), lambda i, j: (layer, 0, j)),
        ],
        out_specs=pl.BlockSpec((tm, tn), lambda i, j: (i, j)),
        out_shape=jax.ShapeDtypeStruct((T, D), _F32),
        scratch_shapes=[pltpu.VMEM((tm, D), _BF)],
        compiler_params=_params("parallel", "arbitrary"),
        name="ple",
    )(x2, norm, p, w_gate, w_proj)


def _slab_of_residue(r):
    return (r % 4) * 4 + r // 4


def _proj_kernel(x_ref, g_ref, w_ref, gain_ref, nat_ref, qkv_ref, h_ref, hp_ref, *, n_nat, n_norm):
    j = pl.program_id(1)
    tm = x_ref.shape[0]
    nsub = tm // PERM_ROWS

    @pl.when(j == 0)
    def _():
        h = _rms(x_ref[...], g_ref[...]).astype(_BF)
        h_ref[...] = h
        nat_ref[...] = _dot(h, w_ref[...])
        lam = lax.broadcasted_iota(jnp.int32, (PERM_ROWS, PERM_ROWS), 0)
        tau = lax.broadcasted_iota(jnp.int32, (PERM_ROWS, PERM_ROWS), 1)
        r = tau % ATTN_SLABS
        perm = (lam == _slab_of_residue(r) * (PERM_ROWS // ATTN_SLABS) + tau // ATTN_SLABS).astype(_BF)
        for s in range(nsub):
            rows = slice(s * PERM_ROWS, (s + 1) * PERM_ROWS)
            hp_ref[rows, :] = _dot(perm, h[rows, :]).astype(_BF)

    @pl.when(jnp.logical_and(j > 0, j < n_nat))
    def _():
        nat_ref[...] = _dot(h_ref[...], w_ref[...])

    def store_qkv(cols, y):
        run = PERM_ROWS // ATTN_SLABS
        for s in range(nsub):
            for slab in range(ATTN_SLABS):
                r0 = s * PERM_ROWS + slab * run
                qkv_ref[slab, s * run:(s + 1) * run, cols] = y[r0:r0 + run, :]

    is_qk = jnp.logical_and(j >= n_nat, j < n_nat + n_norm)

    @pl.when(is_qk)
    def _():
        acc = _dot(hp_ref[...], w_ref[...])
        for c in range(acc.shape[1] // HEAD_DIM):
            sl = slice(c * HEAD_DIM, (c + 1) * HEAD_DIM)
            store_qkv(sl, _rms(acc[:, sl], gain_ref[:, sl]))

    @pl.when(j >= n_nat + n_norm)
    def _():
        store_qkv(slice(None), _dot(hp_ref[...], w_ref[...]))


def _proj(x2, norm, w_in, gain, layer, tm, tn):
    T, D = x2.shape
    n_zd = (COL_GATES - COL_ZD) // tn
    n_nat = n_zd + COL_Q // tn
    n_norm = (COL_V - COL_Q) // tn
    n_all = MIX_COLS // tn
    assert tm % PERM_ROWS == 0 and ATTN_TILE % tm == 0
    per_tile = ATTN_TILE // tm

    def w_col(j):
        return jnp.where(j < n_zd, j + COL_ZD // tn, j - n_zd)

    return pl.pallas_call(
        functools.partial(_proj_kernel, n_nat=n_nat, n_norm=n_norm),
        grid=(T // tm, n_all),
        in_specs=[
            _x_spec(tm, D, T // tm),
            pl.BlockSpec((None, 1, D), lambda i, j: (layer, 0, 0)),
            pl.BlockSpec((None, D, tn), lambda i, j: (layer, 0, w_col(j))),
            pl.BlockSpec((None, 1, tn), lambda i, j: (layer, 0, w_col(j))),
        ],
        out_specs=[
            pl.BlockSpec((tm, tn), lambda i, j: (i, jnp.minimum(j, n_nat - 1))),
            pl.BlockSpec((None, ATTN_SLABS, tm // ATTN_SLABS, tn),
                         lambda i, j: (i // per_tile, 0, i % per_tile, jnp.maximum(j - n_nat, 0))),
        ],
        out_shape=[jax.ShapeDtypeStruct((T, NAT_COLS), _F32),
                   jax.ShapeDtypeStruct((T // ATTN_TILE, ATTN_SLABS, ATTN_BLOCK, QKV_COLS), _F32)],
        scratch_shapes=[pltpu.VMEM((tm, D), _BF), pltpu.VMEM((tm, D), _BF)],
        compiler_params=_params("parallel", "arbitrary"),
        name="proj",
    )(x2, norm, w_in, gain)


def _rglru_kernel(xa_ref, halo_ref, ga_ref, cw_ref, cb_ref, wa_ref, ba_ref, wx_ref, bx_ref,
                  lam_ref, o_ref, carry_ref, sa_ref, sb_ref, xe_ref, *, tb):
    i = pl.program_id(1)

    @pl.when(i == 0)
    def _():
        carry_ref[...] = jnp.zeros_like(carry_ref)

    xe_ref[:CONV_HALO, :] = jnp.where(i == 0, 0.0, halo_ref[...])
    xe_ref[CONV_HALO:, :] = xa_ref[...]
    cw = cw_ref[...]
    xc = cb_ref[...] + cw[0:1] * xa_ref[...]
    for j in range(1, CONV_WIDTH):
        xc = xc + cw[j:j + 1] * xe_ref[pl.ds(CONV_HALO - j, tb), :]
    xcb = xc.astype(_BF)
    r_lin, i_lin = [], []
    for p in range(WIDTH // RNN_BLOCK):
        sl = slice(p * RNN_BLOCK, (p + 1) * RNN_BLOCK)
        r_lin.append(_dot(xcb[:, sl], wa_ref[p]))
        i_lin.append(_dot(xcb[:, sl], wx_ref[p]))
    r = jax.nn.sigmoid(jnp.concatenate(r_lin, axis=1) + ba_ref[...])
    ig = jax.nn.sigmoid(jnp.concatenate(i_lin, axis=1) + bx_ref[...])
    z = -lam_ref[...]
    softplus = jnp.maximum(z, 0.0) + jnp.log1p(jnp.exp(-jnp.abs(z)))
    log_a = (-RG_C * r) * softplus
    a = jnp.exp(log_a)
    y = 1.0 - a * a
    b = jnp.where(y > 0.0, y * lax.rsqrt(y), 0.0) * (ig * xc)
    ng = tb // SCAN_GROUP
    a = a.reshape(ng, SCAN_GROUP, WIDTH)
    b = b.reshape(ng, SCAN_GROUP, WIDTH)
    row = lax.broadcasted_iota(jnp.int32, (1, SCAN_GROUP, 1), 1)
    s = 1
    while s < SCAN_GROUP:
        keep = row >= s
        a_prev = jnp.where(keep, pltpu.roll(a, s, axis=1), 1.0)
        b_prev = jnp.where(keep, pltpu.roll(b, s, axis=1), 0.0)
        b = a * b_prev + b
        a = a * a_prev
        s *= 2
    sa_ref[...] = a
    sb_ref[...] = b

    def chain(g, carry):
        hg = sa_ref[g] * carry + sb_ref[g]
        sb_ref[g] = hg
        return hg[SCAN_GROUP - 1:SCAN_GROUP, :]

    carry_ref[0:1, :] = lax.fori_loop(0, ng, chain, carry_ref[0:1, :], unroll=8)
    h = sb_ref[...].reshape(tb, WIDTH)
    o_ref[...] = (h * jax.nn.gelu(ga_ref[...])).astype(_BF)


def _rglru(proj, cw, cb, wa, ba, wx, bx, lam, layer, B, S, tb):
    nb = S // tb
    hb = tb // CONV_HALO
    vec = pl.BlockSpec((None, 1, WIDTH), lambda b, i: (layer, 0, 0))
    mat = pl.BlockSpec((None, WIDTH // RNN_BLOCK, RNN_BLOCK, RNN_BLOCK), lambda b, i: (layer, 0, 0, 0))
    return pl.pallas_call(
        functools.partial(_rglru_kernel, tb=tb),
        grid=(B, nb),
        in_specs=[
            pl.BlockSpec((tb, WIDTH), lambda b, i: (b * nb + i, NAT_XA // WIDTH)),
            pl.BlockSpec((CONV_HALO, WIDTH),
                         lambda b, i: (jnp.maximum((b * nb + i) * hb - 1, 0), NAT_XA // WIDTH)),
            pl.BlockSpec((tb, WIDTH), lambda b, i: (b * nb + i, NAT_GA // WIDTH)),
            pl.BlockSpec((None, CONV_WIDTH, WIDTH), lambda b, i: (layer, 0, 0)),
            vec, mat, vec, mat, vec, vec,
        ],
        out_specs=pl.BlockSpec((tb, WIDTH), lambda b, i: (b * nb + i, 0)),
        out_shape=jax.ShapeDtypeStruct((B * S, WIDTH), _BF),
        scratch_shapes=[pltpu.VMEM((8, WIDTH), _F32),
                        pltpu.VMEM((tb // SCAN_GROUP, SCAN_GROUP, WIDTH), _F32),
                        pltpu.VMEM((tb // SCAN_GROUP, SCAN_GROUP, WIDTH), _F32),
                        pltpu.VMEM((tb + CONV_HALO, WIDTH), _F32)],
        compiler_params=_params("parallel", "arbitrary"),
        name="rglru",
    )(proj, proj, proj, cw, cb, wa, ba, wx, bx, lam)


def _pool_kernel(xb_ref, halo_ref, pw_ref, sc_ref, o_ref, *, tb):
    i = pl.program_id(1)
    x = xb_ref[...]
    halo = jnp.where(i == 0, 0.0, halo_ref[...])
    xe = jnp.concatenate([halo, x], axis=0)
    pos = i * tb + lax.broadcasted_iota(jnp.int32, (tb, 1), 0)
    for g, win in enumerate(POOL_WINDOWS):
        sl = slice(g * POOL_GROUP, (g + 1) * POOL_GROUP)
        s = xe[:, sl]
        sh = 1
        while sh < win:
            s = s + pltpu.roll(s, sh, axis=0)
            sh *= 2
        cnt = jnp.minimum(pos + 1, win).astype(_F32)
        pooled = s[POOL_HALO:] / cnt - x[:, sl]
        y = _dot(pooled.astype(_BF), pw_ref[g])
        o_ref[:, sl] = (y * sc_ref[:, sl]).astype(_BF)


def _pool(proj, pw, sc, layer, B, S, tb):
    nb = S // tb
    hb = tb // POOL_HALO
    return pl.pallas_call(
        functools.partial(_pool_kernel, tb=tb),
        grid=(B, nb),
        in_specs=[
            pl.BlockSpec((tb, WIDTH), lambda b, i: (b * nb + i, NAT_XB // WIDTH)),
            pl.BlockSpec((POOL_HALO, WIDTH),
                         lambda b, i: (jnp.maximum((b * nb + i) * hb - 1, 0), NAT_XB // WIDTH)),
            pl.BlockSpec((None, len(POOL_WINDOWS), POOL_GROUP, POOL_GROUP), lambda b, i: (layer, 0, 0, 0)),
            pl.BlockSpec((None, 1, WIDTH), lambda b, i: (layer, 0, 0)),
        ],
        out_specs=pl.BlockSpec((tb, WIDTH), lambda b, i: (b * nb + i, 0)),
        out_shape=jax.ShapeDtypeStruct((B * S, WIDTH), _BF),
        compiler_params=_params("parallel", "parallel"),
        name="pool",
    )(proj, proj, pw, sc)


def _attn_order(g):
    slab_res = np.empty(ATTN_SLABS, np.int64)
    for r in range(ATTN_SLABS):
        slab_res[_slab_of_residue(r)] = r
    if g == 0:
        qi = (slab_res[:, None] + ATTN_SLABS * np.arange(8)[None, :]).reshape(-1)
        ki = (slab_res[:, None] + ATTN_SLABS * np.arange(16)[None, :]).reshape(-1)
    elif g == 1:
        qi = (np.arange(4)[:, None] + 4 * np.arange(32)[None, :]).reshape(-1)
        ki = (np.arange(4)[:, None] + 4 * np.arange(64)[None, :]).reshape(-1)
    else:
        qi = np.arange(ATTN_BLOCK)
        ki = np.arange(2 * ATTN_BLOCK)
    return qi, ki


def _attn_bias():
    out = np.empty((2 * len(ATTN_DILATIONS), ATTN_BLOCK, 2 * ATTN_BLOCK), np.float32)
    for g in range(len(ATTN_DILATIONS)):
        qi, ki = _attn_order(g)
        band = (ki[None, :] >= qi[:, None]) & (ki[None, :] <= qi[:, None] + ATTN_BLOCK)
        out[2 * g] = np.where(band, 0.0, NEG_INF)
        out[2 * g + 1] = np.where(band & (ki[None, :] >= ATTN_BLOCK), 0.0, NEG_INF)
    return out


def _attn_kernel(q0_ref, q1_ref, q2_ref, kc_ref, kp_ref, vc_ref, vp_ref, bias_ref, o_ref,
                 o0_ref, o1_ref, o2_ref, l0_ref, l1_ref, l2_ref):
    n = pl.program_id(2)
    NJ = ATTN_BLOCK
    scale = HEAD_DIM ** -0.5
    exp2_scale = scale * 1.4426950408889634
    is_first_tile = (n == 0).astype(jnp.int32)

    shapes = ((ATTN_SLABS, 8), (4, 32), (1, ATTN_BLOCK))
    for g, (q_ref, og_ref, lg_ref) in enumerate(((q0_ref, o0_ref, l0_ref), (q1_ref, o1_ref, l1_ref),
                                                 (q2_ref, o2_ref, l2_ref))):
        ns, nj = shapes[g]
        classes = ATTN_SLABS // ns
        for idx in range(ATTN_TILE // ATTN_BLOCK):
            c = idx % classes
            m = idx // classes
            slabs = slice(c * ns, (c + 1) * ns)
            js = slice(m * nj, (m + 1) * nj)
            q = q_ref[slabs, js, :].reshape(ATTN_BLOCK, HEAD_DIM).astype(_BF)
            if m == 0:
                tail = slice(NJ - nj, NJ)
                k = jnp.concatenate([kp_ref[slabs, tail, :], kc_ref[slabs, js, :]], axis=1)
                v = jnp.concatenate([vp_ref[slabs, tail, :], vc_ref[slabs, js, :]], axis=1)
                bias = bias_ref[2 * g + is_first_tile]
            else:
                both = slice((m - 1) * nj, (m + 1) * nj)
                k = kc_ref[slabs, both, :]
                v = vc_ref[slabs, both, :]
                bias = bias_ref[2 * g]
            k = k.reshape(2 * ATTN_BLOCK, HEAD_DIM).astype(_BF)
            v = v.reshape(2 * ATTN_BLOCK, HEAD_DIM).astype(_BF)
            s = lax.dot_general(q, k, (((1,), (1,)), ((), ())), preferred_element_type=_F32) + bias
            mx = jnp.max(s, axis=-1, keepdims=True)
            p = jnp.exp2((s - mx) * exp2_scale)
            l = jnp.sum(p, axis=-1, keepdims=True)
            o = _dot(p.astype(_BF), v) / l
            lse = mx * scale + jnp.log(l)
            og_ref[slabs, js, :] = o.reshape(ns, nj, HEAD_DIM)
            lg_ref[slabs, js, :] = jnp.broadcast_to(lse, (ATTN_BLOCK, HEAD_DIM)).reshape(ns, nj, HEAD_DIM)

    l0, l1, l2 = l0_ref[...], l1_ref[...], l2_ref[...]
    mx = jnp.maximum(jnp.maximum(l0, l1), l2)
    w0, w1, w2 = jnp.exp(l0 - mx), jnp.exp(l1 - mx), jnp.exp(l2 - mx)
    out = (w0 * o0_ref[...] + w1 * o1_ref[...] + w2 * o2_ref[...]) / (w0 + w1 + w2)
    for r in range(ATTN_SLABS):
        o_ref[pl.ds(r, NJ, stride=ATTN_SLABS), :] = out[_slab_of_residue(r)]


def _attn(qkv, B, S):
    nt = S // ATTN_TILE
    qb, kb, vb = QKV_Q // HEAD_DIM, QKV_K // HEAD_DIM, QKV_V // HEAD_DIM
    blk = (None, ATTN_SLABS, ATTN_BLOCK, HEAD_DIM)

    def cur(col0):
        return pl.BlockSpec(blk, lambda b, h, n: (b * nt + n, 0, 0, col0 + h))

    def prev(col0):
        return pl.BlockSpec(blk, lambda b, h, n: (b * nt + jnp.maximum(n - 1, 0), 0, 0, col0 + h))

    n_bias = 2 * len(ATTN_DILATIONS)
    tile = pltpu.VMEM((ATTN_SLABS, ATTN_BLOCK, HEAD_DIM), _F32)
    return pl.pallas_call(
        _attn_kernel,
        grid=(B, KV_HEADS, nt),
        in_specs=[cur(qb), cur(qb + KV_HEADS), cur(qb + 2 * KV_HEADS),
                  cur(kb), prev(kb), cur(vb), prev(vb),
                  pl.BlockSpec((n_bias, ATTN_BLOCK, 2 * ATTN_BLOCK), lambda b, h, n: (0, 0, 0))],
        out_specs=pl.BlockSpec((ATTN_TILE, HEAD_DIM), lambda b, h, n: (b * nt + n, h)),
        out_shape=jax.ShapeDtypeStruct((B * S, WIDTH), _F32),
        scratch_shapes=[tile, tile, tile, tile, tile, tile],
        compiler_params=_params("parallel", "parallel", "parallel"),
        name="attn",
    )(qkv, qkv, qkv, qkv, qkv, qkv, qkv, jnp.asarray(_attn_bias()))


def _sgu_kernel(z_ref, g_ref, ws_ref, b_ref, o_ref, *, tb):
    gz = jax.nn.gelu(z_ref[...])
    u = gz[:, :WIDTH]
    vv = _rms(gz[:, WIDTH:], g_ref[...]).astype(_BF)
    row = lax.broadcasted_iota(jnp.int32, (SG_CHUNK, SG_CHUNK), 0)
    col = lax.broadcasted_iota(jnp.int32, (SG_CHUNK, SG_CHUNK), 1)
    tri = row >= col
    for g in range(SG_GROUPS):
        w = jnp.where(tri, ws_ref[g], 0.0).astype(_BF)
        cs = slice(g * SG_CHUNK, (g + 1) * SG_CHUNK)
        for c in range(tb // SG_CHUNK):
            rs = slice(c * SG_CHUNK, (c + 1) * SG_CHUNK)
            mixed = _dot(w, vv[rs, cs]) + b_ref[g]
            o_ref[rs, cs] = (u[rs, cs] * mixed).astype(_BF)


def _sgu(proj, sg_norm, sg_w, sg_b, layer, T, tb):
    return pl.pallas_call(
        functools.partial(_sgu_kernel, tb=tb),
        grid=(T // tb,),
        in_specs=[
            pl.BlockSpec((tb, 2 * WIDTH), lambda i: (i, NAT_ZD // (2 * WIDTH))),
            pl.BlockSpec((None, 1, WIDTH), lambda i: (layer, 0, 0)),
            pl.BlockSpec((None, SG_GROUPS, SG_CHUNK, SG_CHUNK), lambda i: (layer, 0, 0, 0)),
            pl.BlockSpec((None, SG_GROUPS, SG_CHUNK, 1), lambda i: (layer, 0, 0, 0)),
        ],
        out_specs=pl.BlockSpec((tb, WIDTH), lambda i: (i, 0)),
        out_shape=jax.ShapeDtypeStruct((T, WIDTH), _BF),
        compiler_params=_params("parallel"),
        name="sgu",
    )(proj, sg_norm, sg_w, sg_b)


def _merge_kernel(x_ref, g_ref, ya_ref, yb_ref, yc_ref, yd_ref, wg0_ref, wg1_ref, wg2_ref, wg3_ref,
                  bg_ref, wb_ref, wo_ref, o_ref, h_ref):
    j = pl.program_id(1)

    @pl.when(j == 0)
    def _():
        h_ref[...] = _rms(x_ref[...], g_ref[...]).astype(_BF)

    def step(first):
        h = h_ref[...]
        merged = None
        for b, (y_ref, wg_ref) in enumerate(((ya_ref, wg0_ref), (yb_ref, wg1_ref),
                                             (yc_ref, wg2_ref), (yd_ref, wg3_ref))):
            gate = jax.nn.sigmoid(_dot(h, wg_ref[...]) + bg_ref[b:b + 1, :])
            term = gate * _dot(y_ref[...].astype(_BF), wb_ref[b])
            merged = term if merged is None else merged + term
        base = x_ref[...] if first else o_ref[...]
        o_ref[...] = base + _dot(merged.astype(_BF), wo_ref[...])

    pl.when(j == 0)(lambda: step(True))
    pl.when(j > 0)(lambda: step(False))


def _merge(x2, norm, ys, w_in, b_gate, w_branch, w_out, layer, tm, tn):
    T, D = x2.shape
    y_spec = pl.BlockSpec((tm, WIDTH), lambda i, j: (i, 0))

    def gate_spec(b):
        col0 = (COL_GATES + b * D) // tn
        return pl.BlockSpec((None, D, tn), lambda i, j: (layer, 0, col0 + j))

    return pl.pallas_call(
        _merge_kernel,
        grid=(T // tm, D // tn),
        in_specs=[
            _x_spec(tm, D, T // tm),
            pl.BlockSpec((None, 1, D), lambda i, j: (layer, 0, 0)),
            y_spec, y_spec, y_spec, y_spec,
            gate_spec(0), gate_spec(1), gate_spec(2), gate_spec(3),
            pl.BlockSpec((None, N_BRANCH, tn), lambda i, j: (layer, 0, j)),
            pl.BlockSpec((None, N_BRANCH, WIDTH, tn), lambda i, j: (layer, 0, 0, j)),
            pl.BlockSpec((None, tn, D), lambda i, j: (layer, j, 0)),
        ],
        out_specs=pl.BlockSpec((tm, D), lambda i, j: (i, 0)),
        out_shape=jax.ShapeDtypeStruct((T, D), _F32),
        scratch_shapes=[pltpu.VMEM((tm, D), _BF)],
        compiler_params=_params("parallel", "arbitrary"),
        name="merge",
    )(x2, norm, *ys, w_in, w_in, w_in, w_in, b_gate, w_branch, w_out)


def _block_diag(w):
    L = w.shape[0]
    per = RNN_BLOCK // RNN_HEAD_DIM
    w = w.reshape(L, RNN_HEADS // per, per, RNN_HEAD_DIM, RNN_HEAD_DIM)
    eye = jnp.eye(per, dtype=w.dtype)
    bd = jnp.einsum('lphij,hk->lphikj', w, eye)
    return bd.reshape(L, RNN_HEADS // per, RNN_BLOCK, RNN_BLOCK).astype(_BF)


def kernel(x, p, ffn1_norm, ffn1_w1, ffn1_w3, ffn1_w2, mix_norm, w_in, b_gate, conv_w, conv_b, rg_wa, rg_ba, rg_wx, rg_bx, rg_lambda, pool_w, pool_scale, q_gain, k_gain, sg_norm, sg_w, sg_b, w_branch, w_out, ffn2_norm, ffn2_w1, ffn2_w3, ffn2_w2, ple_norm, ple_gate_w, ple_proj):
    B, S, D = x.shape
    L = w_in.shape[0]
    T = B * S
    assert D == D_MODEL and S % ATTN_TILE == 0
    tm = 512

    def vec(a):
        return a.reshape(L, 1, a.shape[-1])

    f1 = (ffn1_w1.astype(_BF), ffn1_w3.astype(_BF), ffn1_w2.astype(_BF))
    f2 = (ffn2_w1.astype(_BF), ffn2_w3.astype(_BF), ffn2_w2.astype(_BF))
    w_in_b = w_in.astype(_BF)
    w_branch_b = w_branch.astype(_BF)
    w_out_b = w_out.astype(_BF)
    ple_gate_b = ple_gate_w.astype(_BF)
    ple_proj_b = ple_proj.astype(_BF)
    pool_w_b = pool_w.astype(_BF)
    wa_bd, wx_bd = _block_diag(rg_wa), _block_diag(rg_wx)
    ones = jnp.ones((L, COL_Q), _F32)
    qk_gain = jnp.concatenate(
        [ones, jnp.tile(q_gain, (1, (COL_K - COL_Q) // HEAD_DIM)),
         jnp.tile(k_gain, (1, (COL_V - COL_K) // HEAD_DIM)),
         jnp.ones((L, MIX_COLS - COL_V), _F32)], axis=1).reshape(L, 1, MIX_COLS)
    p2 = p.reshape(L, T, PLE_DIM)
    sg_b4 = sg_b.reshape(L, SG_GROUPS, SG_CHUNK, 1)

    x2 = x.reshape(T, D)
    for i in range(L):
        x2 = _ffn(x2, vec(ffn1_norm), *f1, i, 1024)
        proj, qkv = _proj(x2, vec(mix_norm), w_in_b, qk_gain, i, 1024, 1024)
        ya = _rglru(proj, conv_w, vec(conv_b), wa_bd, vec(rg_ba), wx_bd, vec(rg_bx), vec(rg_lambda),
                    i, B, S, 256)
        yb = _pool(proj, pool_w_b, vec(pool_scale), i, B, S, 512)
        yc = _attn(qkv, B, S)
        yd = _sgu(proj, vec(sg_norm), sg_w, sg_b4, i, T, 512)
        x2 = _merge(x2, vec(mix_norm), (ya, yb, yc, yd), w_in_b, b_gate, w_branch_b, w_out_b, i, tm, 256)
        x2 = _ffn(x2, vec(ffn2_norm), *f2, i, 1024)
        x2 = _ple(x2, vec(ple_norm), p2, ple_gate_b, ple_proj_b, i, 1024, PLE_TILE)
    return x2.reshape(B, S, D)
```
